```python
import math
import jax, jax.numpy as jnp
from jax import lax
import numpy as np

D_MODEL = 2048
BATCH = 4
SEQ = 2048
DEPTH = 4

N_MIXERS = 3
N_A = (DEPTH + 2) // 3
N_B = (DEPTH + 1) // 3
N_C = DEPTH // 3

HG_EXPAND = 128
HG_HEADS = D_MODEL // HG_EXPAND
HG_KEY_DIM = HG_EXPAND
HG_VAL_DIM = D_MODEL // HG_HEADS
HG_KEY_WIDTH = HG_HEADS * HG_KEY_DIM
HG_VAL_WIDTH = HG_HEADS * HG_VAL_DIM
HG_IN_WIDTH = 2 * HG_KEY_WIDTH + 2 * HG_VAL_WIDTH
HG_CHUNK = 64

SB_HEADS = 16
SB_HEAD_DIM = D_MODEL // SB_HEADS
SB_WIDTH = SB_HEADS * SB_HEAD_DIM
SB_BLOCK = 128

POOL_WINDOWS = (2, 4, 8, 16)
N_POOL = len(POOL_WINDOWS)
POOL_GROUP = D_MODEL // N_POOL
MAX_WIN = max(POOL_WINDOWS)

D_FF = 5632
CONV_WIDTH = 3
NORM_EPS = 1e-6

kernel_name = "hybrid_hgrn2_stickbreak_pool_trunk"


def rms_norm(x, g):
    xf = x.astype(jnp.float32)
    y = xf * lax.rsqrt(jnp.mean(xf * xf, axis=-1, keepdims=True) + NORM_EPS)
    return (y * g.astype(jnp.float32)).astype(x.dtype)


def hgrn2_mixer(h, w_in, onorm_g, w_out, lb):
    B, T, _ = h.shape
    nc = T // HG_CHUNK
    proj = h @ w_in
    q, f_pre, i_val, g_pre = jnp.split(
        proj, [HG_KEY_WIDTH, 2 * HG_KEY_WIDTH, 2 * HG_KEY_WIDTH + HG_VAL_WIDTH], axis=-1)
    q = jax.nn.silu(q.astype(jnp.float32))
    lbf = lb.astype(jnp.float32)
    log_f = jnp.logaddexp(jnp.log(lbf), jnp.log1p(-lbf) + jax.nn.log_sigmoid(f_pre.astype(jnp.float32)))
    k = -jnp.expm1(log_f)
    v = i_val.astype(jnp.float32)

    def to_chunks(a, d):
        return a.reshape(B, nc, HG_CHUNK, HG_HEADS, d).transpose(1, 0, 3, 2, 4)

    qc_all = to_chunks(q, HG_KEY_DIM)
    kc_all = to_chunks(k, HG_KEY_DIM)
    vc_all = to_chunks(v, HG_VAL_DIM)
    bc_all = lax.cumsum(to_chunks(log_f, HG_KEY_DIM), axis=3)
    incl = jnp.tril(jnp.ones((HG_CHUNK, HG_CHUNK), dtype=bool))

    def step(S, inp):
        qc, kc, vc, bc = inp
        b_last = bc[:, :, -1:, :]
        o_inter = jnp.einsum('bhck,bhkv->bhcv', qc * jnp.exp(bc), S)
        diff = bc[:, :, :, None, :] - bc[:, :, None, :, :]
        decay = jnp.exp(jnp.where(incl[:, :, None], diff, -jnp.inf))
        scores = jnp.einsum('bhtk,bhsk,bhtsk->bhts', qc, kc, decay)
        o_intra = jnp.einsum('bhts,bhsv->bhtv', scores, vc)
        k_dec = kc * jnp.exp(b_last - bc)
        S_new = jnp.exp(b_last[:, :, 0, :])[..., None] * S + jnp.einsum('bhck,bhcv->bhkv', k_dec, vc)
        return S_new, o_inter + o_intra

    S0 = jnp.zeros((B, HG_HEADS, HG_KEY_DIM, HG_VAL_DIM), jnp.float32)
    _, o = lax.scan(step, S0, (qc_all, kc_all, vc_all, bc_all))
    o = o.transpose(1, 0, 3, 2, 4).reshape(B, T, HG_HEADS, HG_VAL_DIM)
    o = o * lax.rsqrt(jnp.mean(o * o, axis=-1, keepdims=True) + NORM_EPS)
    o = o * onorm_g.astype(jnp.float32).reshape(HG_HEADS, HG_VAL_DIM)
    o = o.reshape(B, T, HG_VAL_WIDTH) * jax.nn.silu(g_pre.astype(jnp.float32))
    return o.astype(h.dtype) @ w_out


def stick_breaking_mixer(h, w_qkv, w_out):
    B, T, _ = h.shape
    qkv = (h @ w_qkv).reshape(B, T, 3, SB_HEADS, SB_HEAD_DIM).astype(jnp.float32)
    q = qkv[:, :, 0] * (SB_HEAD_DIM ** -0.5)
    k = qkv[:, :, 1]
    v = qkv[:, :, 2]
    outs = []
    for blk in range(T // SB_BLOCK):
        t0 = blk * SB_BLOCK
        t1 = t0 + SB_BLOCK
        z = jnp.einsum('bqhd,bshd->bhqs', q[:, t0:t1], k[:, :t1])
        strict = jnp.arange(t1)[None, :] < (t0 + jnp.arange(SB_BLOCK))[:, None]
        sp = jnp.where(strict, jax.nn.softplus(z), 0.0)
        rem = lax.cumsum(sp, axis=3, reverse=True) - sp
        a = jnp.where(strict, jnp.exp(jax.nn.log_sigmoid(z) - rem), 0.0)
        outs.append(jnp.einsum('bhqs,bshd->bqhd', a, v[:, :t1]))
    o = jnp.concatenate(outs, axis=1).reshape(B, T, SB_WIDTH)
    return o.astype(h.dtype) @ w_out


def multiscale_pool_mixer(h, pool_w, pool_scale):
    B, T, _ = h.shape
    hf = h.astype(jnp.float32)
    cs = jnp.pad(lax.cumsum(hf, axis=1), ((0, 0), (MAX_WIN, 0), (0, 0)))
    count_pos = jnp.arange(T)[None, :, None] + 1
    groups = []
    for g, w in enumerate(POOL_WINDOWS):
        c0, c1 = g * POOL_GROUP, (g + 1) * POOL_GROUP
        window_sum = cs[:, MAX_WIN:, c0:c1] - cs[:, MAX_WIN - w:MAX_WIN - w + T, c0:c1]
        count = jnp.minimum(count_pos, w).astype(jnp.float32)
        groups.append(window_sum / count - hf[:, :, c0:c1])
    p = jnp.stack(groups, axis=2)
    y = jnp.einsum('btgc,gcd->btgd', p, pool_w.astype(jnp.float32)).reshape(B, T, D_MODEL)
    return (y * pool_scale.astype(jnp.float32)).astype(h.dtype)


def conv_gated_ffn(h, w_up, conv_w, conv_b, w_down):
    u = h @ w_up
    c = u.shape[-1]
    uc = lax.conv_general_dilated(
        u, conv_w[:, None, :].astype(u.dtype), window_strides=(1,),
        padding=[(CONV_WIDTH - 1, 0)], dimension_numbers=('NWC', 'WIO', 'NWC'),
        feature_group_count=c) + conv_b
    gate, val = jnp.split(uc, 2, axis=-1)
    return (jax.nn.silu(gate) * val) @ w_down


def setup_inputs(seed: int = 0) -> dict:
    key = jax.random.key(seed)
    ks = jax.random.split(key, 14)
    f32 = jnp.float32

    def nrm(k, shape, scale):
        return jax.random.normal(k, shape, f32) * scale

    return {
        "x": nrm(ks[0], (BATCH, SEQ, D_MODEL), 1.0),
        "norm_g": 1.0 + nrm(ks[1], (DEPTH, 4, D_MODEL), 0.05),
        "hgrn_lb_logits": nrm(ks[2], (DEPTH, HG_KEY_WIDTH), 0.5),
        "hgrn_w_in": nrm(ks[3], (N_A, D_MODEL, HG_IN_WIDTH), D_MODEL ** -0.5),
        "hgrn_onorm_g": 1.0 + nrm(ks[4], (N_A, HG_VAL_WIDTH), 0.05),
        "hgrn_w_out": nrm(ks[5], (N_A, HG_VAL_WIDTH, D_MODEL), HG_VAL_WIDTH ** -0.5),
        "sba_w_qkv": nrm(ks[6], (N_B, D_MODEL, 3 * SB_WIDTH), D_MODEL ** -0.5),
        "sba_w_out": nrm(ks[7], (N_B, SB_WIDTH, D_MODEL), SB_WIDTH ** -0.5),
        "pool_w": nrm(ks[8], (N_C, N_POOL, POOL_GROUP, POOL_GROUP), POOL_GROUP ** -0.5),
        "pool_scale": 1.0 + nrm(ks[9], (N_C, D_MODEL), 0.1),
        "ffn_w_up": nrm(ks[10], (DEPTH, D_MODEL, 2 * D_FF), D_MODEL ** -0.5),
        "ffn_conv_w": nrm(ks[11], (DEPTH, CONV_WIDTH, 2 * D_FF), CONV_WIDTH ** -0.5),
        "ffn_conv_b": nrm(ks[12], (DEPTH, 2 * D_FF), 0.02),
        "ffn_w_down": nrm(ks[13], (DEPTH, D_FF, D_MODEL), D_FF ** -0.5),
    }


def reference(x, norm_g, hgrn_lb_logits, hgrn_w_in, hgrn_onorm_g, hgrn_w_out,
              sba_w_qkv, sba_w_out, pool_w, pool_scale,
              ffn_w_up, ffn_conv_w, ffn_conv_b, ffn_w_down):
    lb_soft = jax.nn.softmax(hgrn_lb_logits.astype(jnp.float32), axis=0)
    lower_bounds = jnp.concatenate(
        [jnp.zeros_like(lb_soft[:1]), lax.cumsum(lb_soft[1:], axis=0)], axis=0)
    h = x
    for i in range(DEPTH):
        kind, j = i % N_MIXERS, i // N_MIXERS
        u = rms_norm(h, norm_g[i, 0])
        if kind == 0:
            m = hgrn2_mixer(u, hgrn_w_in[j], hgrn_onorm_g[j], hgrn_w_out[j], lower_bounds[i])
        elif kind == 1:
            m = stick_breaking_mixer(u, sba_w_qkv[j], sba_w_out[j])
        else:
            m = multiscale_pool_mixer(u, pool_w[j], pool_scale[j])
        h = h + rms_norm(m, norm_g[i, 1])
        u = rms_norm(h, norm_g[i, 2])
        f = conv_gated_ffn(u, ffn_w_up[i], ffn_conv_w[i], ffn_conv_b[i], ffn_w_down[i])
        h = h + rms_norm(f, norm_g[i, 3])
    return h
```

```python
import functools

import numpy as np
import jax
import jax.numpy as jnp
from jax import lax
from jax.experimental import pallas as pl
from jax.experimental.pallas import tpu as pltpu

F32 = jnp.float32
BF16 = jnp.bfloat16

D_MODEL = 2048
NORM_EPS = 1e-6
HEAD_DIM = 128
N_HEADS = D_MODEL // HEAD_DIM
HG_CHUNK = 64
HG_LEVELS = (32, 16, 8, 4, 2, 1)
POOL_WINDOWS = (2, 4, 8, 16)
POOL_GROUP = D_MODEL // len(POOL_WINDOWS)
POOL_HIST = 16
D_FF = 5632
CONV_HIST = 8
VMEM_LIMIT_BYTES = 56 * 1024 * 1024


def _cparams(n_axes):
    return pltpu.CompilerParams(
        dimension_semantics=("arbitrary",) * n_axes,
        vmem_limit_bytes=VMEM_LIMIT_BYTES)


def _rms(x, g):
    ms = jnp.mean(x * x, axis=-1, keepdims=True)
    return x * lax.rsqrt(ms + NORM_EPS) * g


def _silu(x):
    return x / (1.0 + jnp.exp(-x))


def _dot(a, b):
    return jnp.dot(a, b, preferred_element_type=F32)


def _dot_nt(a, b):
    return lax.dot_general(a, b, (((1,), (1,)), ((), ())), preferred_element_type=F32)


def _dot_tn(a, b):
    return lax.dot_general(a, b, (((0,), (0,)), ((), ())), preferred_element_type=F32)


def _split_bf16(x):
    hi = x.astype(BF16)
    lo = (x - hi.astype(F32)).astype(BF16)
    return hi, lo


def _norm_matmul_kernel(x_ref, g_ref, w_ref, o_ref, xn_ref):
    @pl.when(pl.program_id(1) == 0)
    def _():
        xn_ref[...] = _rms(x_ref[...], g_ref[...]).astype(BF16)

    o_ref[...] = _dot(xn_ref[...], w_ref[...]).astype(o_ref.dtype)


def _norm_matmul(x, g, w, out_dtype, tm=512, tn=1024):
    m, d = x.shape
    n = w.shape[1]
    return pl.pallas_call(
        _norm_matmul_kernel,
        grid=(m // tm, n // tn),
        in_specs=[
            pl.BlockSpec((tm, d), lambda i, j: (i, 0)),
            pl.BlockSpec((1, d), lambda i, j: (0, 0)),
            pl.BlockSpec((d, tn), lambda i, j: (0, j)),
        ],
        out_specs=pl.BlockSpec((tm, tn), lambda i, j: (i, j)),
        out_shape=jax.ShapeDtypeStruct((m, n), out_dtype),
        scratch_shapes=[pltpu.VMEM((tm, d), BF16)],
        compiler_params=_cparams(2),
        name="norm_matmul",
    )(x, g, w)


def _proj_residual_kernel(a_ref, w_ref, g_ref, h_ref, o_ref):
    y = _dot(a_ref[...], w_ref[...])
    o_ref[...] = h_ref[...] + _rms(y, g_ref[...])


def _proj_residual(a, w, g, h, tm=512):
    m, k = a.shape
    d = w.shape[1]
    return pl.pallas_call(
        _proj_residual_kernel,
        grid=(m // tm,),
        in_specs=[
            pl.BlockSpec((tm, k), lambda i: (i, 0)),
            pl.BlockSpec((k, d), lambda i: (0, 0)),
            pl.BlockSpec((1, d), lambda i: (0, 0)),
            pl.BlockSpec((tm, d), lambda i: (i, 0)),
        ],
        out_specs=pl.BlockSpec((tm, d), lambda i: (i, 0)),
        out_shape=jax.ShapeDtypeStruct((m, d), F32),
        compiler_params=_cparams(1),
        name="proj_residual",
    )(a, w, g, h)


def _hgrn_tables():
    c = HG_CHUNK
    t = np.arange(c)[:, None]
    j = np.arange(c)[None, :]
    blocks = [(j <= t), (j > t)]
    masks = [(t == j)]
    for m in HG_LEVELS:
        ref = (t // (2 * m)) * (2 * m) + m - 1
        second = (t % (2 * m)) >= m
        blocks.append(np.where(second, (j > ref) & (j <= t), (j > t) & (j <= ref)))
        masks.append(((t // (2 * m)) == (j // (2 * m))) & second & ((j % (2 * m)) < m))
    a = np.concatenate(blocks, axis=0).astype(np.float32)
    msk = np.stack(masks, axis=0).astype(np.float32)
    return a, msk


def _hgrn_kernel(q_ref, f_ref, i_ref, g_ref, al_ref, cl_ref, on_ref, a_ref, m_ref,
                 o_ref, st_ref, *, rows_per_step, heads_per_step):
    c = HG_CHUNK

    @pl.when(pl.program_id(2) == 0)
    def _():
        st_ref[...] = jnp.zeros_like(st_ref)

    def chunk(ci, carry):
        rows = pl.ds(pl.multiple_of(ci * c, c), c)
        for hh in range(heads_per_step):
            cols = slice(hh * HEAD_DIM, (hh + 1) * HEAD_DIM)
            fp = f_ref[rows, cols]
            ls = jnp.minimum(fp, 0.0) - jnp.log1p(jnp.exp(-jnp.abs(fp)))
            x2 = cl_ref[:, cols] + ls
            al = al_ref[:, cols]
            logf = jnp.maximum(al, x2) + jnp.log1p(jnp.exp(-jnp.abs(al - x2)))
            kk = 1.0 - jnp.exp(logf)
            hi, lo = _split_bf16(logf)
            nd2 = _dot(a_ref[...], jnp.concatenate([hi, lo], axis=1))
            e = jnp.exp(nd2[:, :HEAD_DIM] + nd2[:, HEAD_DIM:])
            q = _silu(q_ref[rows, cols])
            v = i_ref[rows, cols].astype(BF16)
            p = _dot_nt(q.astype(BF16), kk.astype(BF16)) * m_ref[0]
            for l in range(len(HG_LEVELS)):
                el = e[(2 + l) * c:(3 + l) * c]
                p = p + _dot_nt((q * el).astype(BF16), (kk * el).astype(BF16)) * m_ref[l + 1]
            st = st_ref[hh]
            o = _dot(p.astype(BF16), v) + _dot_nt((q * e[0:c]).astype(BF16), st.astype(BF16))
            kdec = (kk * e[c:2 * c]).astype(BF16)
            st_ref[hh] = st * e[c - 1:c] + _dot_tn(v, kdec)
            o = _rms(o, on_ref[:, cols]) * _silu(g_ref[rows, cols])
            o_ref[rows, cols] = o.astype(o_ref.dtype)
        return carry

    lax.fori_loop(0, rows_per_step // c, chunk, 0)


def _hgrn_core(proj, log_lb, log1m_lb, onorm_g, batch, seq, rows_per_step=512, heads_per_step=4):
    m = proj.shape[0]
    tr = min(rows_per_step, seq)
    hb = heads_per_step
    w = hb * HEAD_DIM
    sec = D_MODEL // w
    rt = seq // tr
    a_np, m_np = _hgrn_tables()
    a_tab = jnp.asarray(a_np, BF16)
    m_tab = jnp.asarray(m_np, F32)

    def sect(s):
        return pl.BlockSpec((tr, w), lambda b, h, r, s=s: (b * rt + r, s * sec + h))

    vec = pl.BlockSpec((1, w), lambda b, h, r: (0, h))
    kern = functools.partial(_hgrn_kernel, rows_per_step=tr, heads_per_step=hb)
    return pl.pallas_call(
        kern,
        grid=(batch, N_HEADS // hb, rt),
        in_specs=[sect(0), sect(1), sect(2), sect(3), vec, vec, vec,
                  pl.BlockSpec(a_np.shape, lambda b, h, r: (0, 0)),
                  pl.BlockSpec(m_np.shape, lambda b, h, r: (0, 0, 0))],
        out_specs=pl.BlockSpec((tr, w), lambda b, h, r: (b * rt + r, h)),
        out_shape=jax.ShapeDtypeStruct((m, D_MODEL), BF16),
        scratch_shapes=[pltpu.VMEM((hb, HEAD_DIM, HEAD_DIM), F32)],
        compiler_params=_cparams(3),
        name="hgrn_core",
    )(proj, proj, proj, proj, log_lb, log1m_lb, onorm_g, a_tab, m_tab)


def _sba_kernel(q_ref, k_ref, v_ref, uo_ref, o_ref, c_ref, acc_ref,
                *, block_q, block_k, heads_per_step, scale):
    qi = pl.program_id(2)
    nblk = (qi + 1) * (block_q // block_k)
    t_idx = qi * block_q + lax.broadcasted_iota(jnp.int32, (block_q, block_k), 0)
    lane = lax.broadcasted_iota(jnp.int32, (block_q, block_k), 1)
    c_ref[...] = jnp.zeros_like(c_ref)
    acc_ref[...] = jnp.zeros_like(acc_ref)

    def body(it, carry):
        s0 = pl.multiple_of((nblk - 1 - it) * block_k, block_k)
        strict = (s0 + lane) < t_idx
        for hh in range(heads_per_step):
            cols = slice(hh * HEAD_DIM, (hh + 1) * HEAD_DIM)
            z = _dot_nt(q_ref[:, cols], k_ref[pl.ds(s0, block_k), cols]) * scale
            sp = jnp.maximum(z, 0.0) + jnp.log1p(jnp.exp(-jnp.abs(z)))
            sp = jnp.where(strict, sp, 0.0)
            hi, lo = _split_bf16(sp)
            cs = _dot(hi, uo_ref[...]) + _dot(lo, uo_ref[...])
            a = jnp.where(strict, jnp.exp(z - cs[:, :block_k] - c_ref[hh]), 0.0)
            acc_ref[hh] += _dot(a.astype(BF16), v_ref[pl.ds(s0, block_k), cols])
            c_ref[hh] += cs[:, block_k:]
        return carry

    lax.fori_loop(0, nblk, body, 0)
    for hh in range(heads_per_step):
        o_ref[:, hh * HEAD_DIM:(hh + 1) * HEAD_DIM] = acc_ref[hh].astype(o_ref.dtype)


def _sba_core(qkv, batch, seq, block_q=256, block_k=128, heads_per_step=2):
    m = qkv.shape[0]
    bq = min(block_q, seq)
    hb = heads_per_step
    w = hb * HEAD_DIM
    sec = D_MODEL // w
    qt = seq // bq
    jj = np.arange(block_k)
    uo = np.concatenate([(jj[:, None] >= jj[None, :]), np.ones((block_k, block_k), bool)], axis=1)
    uo = jnp.asarray(uo.astype(np.float32), BF16)
    kern = functools.partial(_sba_kernel, block_q=bq, block_k=block_k, heads_per_step=hb,
                             scale=float(HEAD_DIM) ** -0.5)
    return pl.pallas_call(
        kern,
        grid=(batch, N_HEADS // hb, qt),
        in_specs=[
            pl.BlockSpec((bq, w), lambda b, h, i: (b * qt + i, h)),
            pl.BlockSpec((seq, w), lambda b, h, i: (b, sec + h)),
            pl.BlockSpec((seq, w), lambda b, h, i: (b, 2 * sec + h)),
            pl.BlockSpec(uo.shape, lambda b, h, i: (0, 0)),
        ],
        out_specs=pl.BlockSpec((bq, w), lambda b, h, i: (b * qt + i, h)),
        out_shape=jax.ShapeDtypeStruct((m, D_MODEL), BF16),
        scratch_shapes=[pltpu.VMEM((hb, bq, block_k), F32), pltpu.VMEM((hb, bq, HEAD_DIM), F32)],
        compiler_params=_cparams(3),
        name="sba_core",
    )(qkv, qkv, qkv, uo)


def _pool_kernel(h_ref, g0_ref, pw_ref, ps_ref, g1_ref, o_ref, ubuf_ref, y_ref,
                 *, tm, tiles_per_seq):
    ti = pl.program_id(0) % tiles_per_seq
    hist = POOL_HIST

    @pl.when(ti == 0)
    def _():
        ubuf_ref[0:hist, :] = jnp.zeros((hist, D_MODEL), F32)

    h = h_ref[...]
    ubuf_ref[hist:hist + tm, :] = _rms(h, g0_ref[...])
    pos = ti * tm + lax.broadcasted_iota(jnp.int32, (tm, POOL_GROUP), 0) + 1
    for gi, win in enumerate(POOL_WINDOWS):
        cols = slice(gi * POOL_GROUP, (gi + 1) * POOL_GROUP)
        u = ubuf_ref[hist:hist + tm, cols]
        ws = u
        for dlt in range(1, win):
            ws = ws + ubuf_ref[hist - dlt:hist - dlt + tm, cols]
        p = ws / jnp.minimum(pos, win).astype(F32) - u
        y_ref[:, cols] = _dot(p.astype(BF16), pw_ref[gi]) * ps_ref[:, cols]
    o_ref[...] = h + _rms(y_ref[...], g1_ref[...])
    ubuf_ref[0:hist, :] = ubuf_ref[tm:tm + hist, :]


def _pool_layer(h, g0, pool_w, pool_scale, g1, seq, tm=256):
    m, d = h.shape
    tm = min(tm, seq)
    kern = functools.partial(_pool_kernel, tm=tm, tiles_per_seq=seq // tm)
    row = pl.BlockSpec((1, d), lambda i: (0, 0))
    return pl.pallas_call(
        kern,
        grid=(m // tm,),
        in_specs=[
            pl.BlockSpec((tm, d), lambda i: (i, 0)),
            row,
            pl.BlockSpec(pool_w.shape, lambda i: (0, 0, 0)),
            row,
            row,
        ],
        out_specs=pl.BlockSpec((tm, d), lambda i: (i, 0)),
        out_shape=jax.ShapeDtypeStruct((m, d), F32),
        scratch_shapes=[pltpu.VMEM((tm + POOL_HIST, d), F32), pltpu.VMEM((tm, d), F32)],
        compiler_params=_cparams(1),
        name="pool_layer",
    )(h, g0, pool_w, pool_scale, g1)


def _ffn_kernel(h_ref, g2_ref, wg_ref, wv_ref, cwg_ref, cwv_ref, cbg_ref, cbv_ref, wd_ref, g3_ref,
                o_ref, xn_ref, acc_ref, ug_ref, uv_ref, carry_ref, *, tm, tf, tiles_per_seq):
    i = pl.program_id(0)
    j = pl.program_id(1)
    hist = CONV_HIST
    first = (i % tiles_per_seq) == 0

    @pl.when(j == 0)
    def _():
        xn_ref[...] = _rms(h_ref[...], g2_ref[...]).astype(BF16)
        acc_ref[...] = jnp.zeros_like(acc_ref)

    def conv(u_ref, w_ref, cw_ref, cb_ref, lane0):
        u = _dot(xn_ref[...], w_ref[...])
        prev = carry_ref[j, :, lane0:lane0 + tf]
        u_ref[0:hist, :] = jnp.where(first, 0.0, prev)
        u_ref[hist:hist + tm, :] = u
        carry_ref[j, :, lane0:lane0 + tf] = u[tm - hist:tm, :]
        return (cw_ref[0:1, :] * u_ref[hist - 2:hist - 2 + tm, :]
                + cw_ref[1:2, :] * u_ref[hist - 1:hist - 1 + tm, :]
                + cw_ref[2:3, :] * u + cb_ref[...])

    gate = conv(ug_ref, wg_ref, cwg_ref, cbg_ref, 0)
    val = conv(uv_ref, wv_ref, cwv_ref, cbv_ref, tf)
    act = (_silu(gate) * val).astype(BF16)
    acc_ref[...] += _dot(act, wd_ref[...])

    @pl.when(j == pl.num_programs(1) - 1)
    def _():
        o_ref[...] = h_ref[...] + _rms(acc_ref[...], g3_ref[...])


def _ffn_layer(h, g2, w_up, conv_w, conv_b, w_down, g3, seq, tm=512, tf=512):
    m, d = h.shape
    tm = min(tm, seq)
    nf = D_FF // tf
    kern = functools.partial(_ffn_kernel, tm=tm, tf=tf, tiles_per_seq=seq // tm)
    row = pl.BlockSpec((1, d), lambda i, j: (0, 0))
    return pl.pallas_call(
        kern,
        grid=(m // tm, nf),
        in_specs=[
            pl.BlockSpec((tm, d), lambda i, j: (i, 0)),
            row,
            pl.BlockSpec((d, tf), lambda i, j: (0, j)),
            pl.BlockSpec((d, tf), lambda i, j: (0, nf + j)),
            pl.BlockSpec((3, tf), lambda i, j: (0, j)),
            pl.BlockSpec((3, tf), lambda i, j: (0, nf + j)),
            pl.BlockSpec((1, tf), lambda i, j: (0, j)),
            pl.BlockSpec((1, tf), lambda i, j: (0, nf + j)),
            pl.BlockSpec((tf, d), lambda i, j: (j, 0)),
            row,
        ],
        out_specs=pl.BlockSpec((tm, d), lambda i, j: (i, 0)),
        out_shape=jax.ShapeDtypeStruct((m, d), F32),
        scratch_shapes=[
            pltpu.VMEM((tm, d), BF16),
            pltpu.VMEM((tm, d), F32),
            pltpu.VMEM((tm + CONV_HIST, tf), F32),
            pltpu.VMEM((tm + CONV_HIST, tf), F32),
            pltpu.VMEM((nf, CONV_HIST, 2 * tf), F32),
        ],
        compiler_params=_cparams(2),
        name="ffn_layer",
    )(h, g2, w_up, w_up, conv_w, conv_w, conv_b, conv_b, w_down, g3)


def kernel(x, norm_g, hgrn_lb_logits, hgrn_w_in, hgrn_onorm_g, hgrn_w_out, sba_w_qkv, sba_w_out,
           pool_w, pool_scale, ffn_w_up, ffn_conv_w, ffn_conv_b, ffn_w_down):
    batch, seq, d = x.shape
    depth = norm_g.shape[0]
    lb_soft = jax.nn.softmax(hgrn_lb_logits.astype(F32), axis=0)
    lower = jnp.concatenate([jnp.zeros_like(lb_soft[:1]), jnp.cumsum(lb_soft[1:], axis=0)], axis=0)
    log_lb = jnp.log(lower)
    log1m_lb = jnp.log1p(-lower)

    h = x.reshape(batch * seq, d)
    for i in range(depth):
        kind, j = i % 3, i // 3
        g = norm_g[i].reshape(4, 1, d)
        if kind == 0:
            proj = _norm_matmul(h, g[0], hgrn_w_in[j].astype(BF16), F32)
            mix = _hgrn_core(proj, log_lb[i:i + 1], log1m_lb[i:i + 1], hgrn_onorm_g[j][None, :], batch, seq)
            h = _proj_residual(mix, hgrn_w_out[j].astype(BF16), g[1], h)
        elif kind == 1:
            qkv = _norm_matmul(h, g[0], sba_w_qkv[j].astype(BF16), BF16)
            mix = _sba_core(qkv, batch, seq)
            h = _proj_residual(mix, sba_w_out[j].astype(BF16), g[1], h)
        else:
            h = _pool_layer(h, g[0], pool_w[j].astype(BF16), pool_scale[j][None, :], g[1], seq)
        h = _ffn_layer(h, g[2], ffn_w_up[i].astype(BF16), ffn_conv_w[i], ffn_conv_b[i][None, :],
                       ffn_w_down[i].astype(BF16), g[3], seq)
    return h.reshape(batch, seq, d)
```

```python
import functools

import numpy as np
import jax
import jax.numpy as jnp
from jax import lax
from jax.experimental import pallas as pl
from jax.experimental.pallas import tpu as pltpu

F32 = jnp.float32
BF16 = jnp.bfloat16

D_MODEL = 2048
NORM_EPS = 1e-6
LOG2_E = 1.4426950408889634
HEAD_DIM = 128
N_HEADS = D_MODEL // HEAD_DIM
HG_CHUNK = 64
HG_LEVELS = (32, 16, 8, 4, 2, 1)
POOL_WINDOWS = (2, 4, 8, 16)
POOL_GROUP = D_MODEL // len(POOL_WINDOWS)
POOL_HIST = 16
D_FF = 5632
CONV_HIST = 8
VMEM_LIMIT_BYTES = 56 * 1024 * 1024


def _cparams(n_axes):
    return pltpu.CompilerParams(
        dimension_semantics=("arbitrary",) * n_axes,
        vmem_limit_bytes=VMEM_LIMIT_BYTES)


def _rms(x, g):
    ms = jnp.mean(x * x, axis=-1, keepdims=True)
    return x * lax.rsqrt(ms + NORM_EPS) * g


def _silu(x):
    return x / (1.0 + jnp.exp(-x))


def _dot(a, b):
    return jnp.dot(a, b, preferred_element_type=F32)


def _dot_nt(a, b):
    return lax.dot_general(a, b, (((1,), (1,)), ((), ())), preferred_element_type=F32)


def _dot_tn(a, b):
    return lax.dot_general(a, b, (((0,), (0,)), ((), ())), preferred_element_type=F32)


def _split_bf16(x):
    hi = x.astype(BF16)
    lo = (x - hi.astype(F32)).astype(BF16)
    return hi, lo


def _norm_matmul_kernel(x_ref, g_ref, w_ref, o_ref, xn_ref, *, scaled_blocks, scale):
    @pl.when(pl.program_id(1) == 0)
    def _():
        xn_ref[...] = _rms(x_ref[...], g_ref[...]).astype(BF16)

    y = _dot(xn_ref[...], w_ref[...])
    if scaled_blocks:
        y = y * jnp.where(pl.program_id(1) < scaled_blocks, scale, 1.0)
    o_ref[...] = y.astype(o_ref.dtype)


def _norm_matmul(x, g, w, out_dtype, scaled_cols=0, scale=1.0, tm=512, tn=1024):
    m, d = x.shape
    n = w.shape[1]
    kern = functools.partial(_norm_matmul_kernel, scaled_blocks=scaled_cols // tn, scale=scale)
    return pl.pallas_call(
        kern,
        grid=(m // tm, n // tn),
        in_specs=[
            pl.BlockSpec((tm, d), lambda i, j: (i, 0)),
            pl.BlockSpec((1, d), lambda i, j: (0, 0)),
            pl.BlockSpec((d, tn), lambda i, j: (0, j)),
        ],
        out_specs=pl.BlockSpec((tm, tn), lambda i, j: (i, j)),
        out_shape=jax.ShapeDtypeStruct((m, n), out_dtype),
        scratch_shapes=[pltpu.VMEM((tm, d), BF16)],
        compiler_params=_cparams(2),
        name="norm_matmul",
    )(x, g, w)


def _proj_residual_kernel(a_ref, w_ref, g_ref, h_ref, o_ref):
    y = _dot(a_ref[...], w_ref[...])
    o_ref[...] = h_ref[...] + _rms(y, g_ref[...])


def _proj_residual(a, w, g, h, tm=512):
    m, k = a.shape
    d = w.shape[1]
    return pl.pallas_call(
        _proj_residual_kernel,
        grid=(m // tm,),
        in_specs=[
            pl.BlockSpec((tm, k), lambda i: (i, 0)),
            pl.BlockSpec((k, d), lambda i: (0, 0)),
            pl.BlockSpec((1, d), lambda i: (0, 0)),
            pl.BlockSpec((tm, d), lambda i: (i, 0)),
        ],
        out_specs=pl.BlockSpec((tm, d), lambda i: (i, 0)),
        out_shape=jax.ShapeDtypeStruct((m, d), F32),
        compiler_params=_cparams(1),
        name="proj_residual",
    )(a, w, g, h)


def _hgrn_tables():
    c = HG_CHUNK
    t = np.arange(c)[:, None]
    j = np.arange(c)[None, :]
    blocks = [(j <= t), (j > t)]
    masks = [(t == j)]
    for m in HG_LEVELS:
        ref = (t // (2 * m)) * (2 * m) + m - 1
        second = (t % (2 * m)) >= m
        blocks.append(np.where(second, (j > ref) & (j <= t), (j > t) & (j <= ref)))
        masks.append(((t // (2 * m)) == (j // (2 * m))) & second & ((j % (2 * m)) < m))
    a = np.concatenate(blocks, axis=0).astype(np.float32)
    a = np.concatenate([a, a], axis=1)
    msk = np.stack(masks, axis=0).astype(np.float32)
    return a, msk


def _hgrn_kernel(q_ref, f_ref, i_ref, g_ref, al_ref, cl_ref, on_ref, a_ref, m_ref,
                 o_ref, st_ref, *, rows_per_step, heads_per_step):
    c = HG_CHUNK

    @pl.when(pl.program_id(2) == 0)
    def _():
        st_ref[...] = jnp.zeros_like(st_ref)

    def chunk(ci, carry):
        rows = pl.ds(pl.multiple_of(ci * c, c), c)
        heads = range(heads_per_step)
        cols = [slice(hh * HEAD_DIM, (hh + 1) * HEAD_DIM) for hh in heads]
        loaded = [(f_ref[rows, cs], q_ref[rows, cs], i_ref[rows, cs], g_ref[rows, cs], st_ref[hh])
                  for hh, cs in zip(heads, cols)]
        results = []
        for hh in heads:
            fp, qp, ip, gp, st = loaded[hh]
            ls = jnp.minimum(fp, 0.0) - jnp.log1p(jnp.exp(-jnp.abs(fp)))
            x2 = cl_ref[:, cols[hh]] + ls
            al = al_ref[:, cols[hh]]
            logf = jnp.maximum(al, x2) + jnp.log1p(jnp.exp(-jnp.abs(al - x2)))
            kk = 1.0 - jnp.exp(logf)
            hi, lo = _split_bf16(logf)
            e = jnp.exp(_dot(a_ref[...], jnp.concatenate([hi, lo], axis=0)))
            q = _silu(qp)
            v = ip.astype(BF16)
            p = _dot_nt(q.astype(BF16), kk.astype(BF16)) * m_ref[0]
            for l in range(len(HG_LEVELS)):
                el = e[(2 + l) * c:(3 + l) * c]
                p = p + _dot_nt((q * el).astype(BF16), (kk * el).astype(BF16)) * m_ref[l + 1]
            o = _dot(p.astype(BF16), v) + _dot_nt((q * e[0:c]).astype(BF16), st.astype(BF16))
            kdec = (kk * e[c:2 * c]).astype(BF16)
            st_new = st * e[c - 1:c] + _dot_tn(v, kdec)
            o = _rms(o, on_ref[:, cols[hh]]) * _silu(gp)
            results.append((o.astype(o_ref.dtype), st_new))
        for hh in heads:
            o_ref[rows, cols[hh]] = results[hh][0]
            st_ref[hh] = results[hh][1]
        return carry

    lax.fori_loop(0, rows_per_step // c, chunk, 0)


def _hgrn_core(proj, log_lb, log1m_lb, onorm_g, batch, seq, rows_per_step=512, heads_per_step=8):
    m = proj.shape[0]
    tr = min(rows_per_step, seq)
    hb = heads_per_step
    w = hb * HEAD_DIM
    sec = D_MODEL // w
    rt = seq // tr
    a_np, m_np = _hgrn_tables()
    a_tab = jnp.asarray(a_np, BF16)
    m_tab = jnp.asarray(m_np, F32)

    def sect(s):
        return pl.BlockSpec((tr, w), lambda b, h, r, s=s: (b * rt + r, s * sec + h))

    vec = pl.BlockSpec((1, w), lambda b, h, r: (0, h))
    kern = functools.partial(_hgrn_kernel, rows_per_step=tr, heads_per_step=hb)
    return pl.pallas_call(
        kern,
        grid=(batch, N_HEADS // hb, rt),
        in_specs=[sect(0), sect(1), sect(2), sect(3), vec, vec, vec,
                  pl.BlockSpec(a_np.shape, lambda b, h, r: (0, 0)),
                  pl.BlockSpec(m_np.shape, lambda b, h, r: (0, 0, 0))],
        out_specs=pl.BlockSpec((tr, w), lambda b, h, r: (b * rt + r, h)),
        out_shape=jax.ShapeDtypeStruct((m, D_MODEL), BF16),
        scratch_shapes=[pltpu.VMEM((hb, HEAD_DIM, HEAD_DIM), F32)],
        compiler_params=_cparams(3),
        name="hgrn_core",
    )(proj, proj, proj, proj, log_lb, log1m_lb, onorm_g, a_tab, m_tab)


def _sba_kernel(q_ref, k_ref, v_ref, uo_ref, o_ref, c_ref, acc_ref, *, block_q, block_k, heads_per_step):
    qi = pl.program_id(2)
    heads = range(heads_per_step)
    cols = [slice(hh * HEAD_DIM, (hh + 1) * HEAD_DIM) for hh in heads]
    nkb = block_q // block_k

    def chunk(s0, diagonal, state):
        if diagonal:
            row = lax.broadcasted_iota(jnp.int32, (block_q, block_k), 0)
            lane = lax.broadcasted_iota(jnp.int32, (block_q, block_k), 1)
        out = []
        for hh in heads:
            c, acc = state[hh]
            z = _dot_nt(q_ref[:, cols[hh]], k_ref[pl.ds(s0, block_q), cols[hh]])
            a_blocks = [None] * nkb
            for kb in reversed(range(nkb)):
                zb = z[:, kb * block_k:(kb + 1) * block_k]
                nz = pltpu.bitcast(pltpu.bitcast(zb, jnp.int32) | jnp.int32(-2 ** 31), F32)
                lg = jnp.log2(1.0 + jnp.exp2(nz))
                sp = jnp.maximum(zb, 0.0) + lg
                ls = jnp.minimum(zb, 0.0) - lg
                if diagonal:
                    strict = (kb * block_k + lane) < row
                    sp = jnp.where(strict, sp, 0.0)
                cs = _dot(sp.astype(BF16), uo_ref[...])
                a = jnp.exp2(ls - cs[:, :block_k] - c)
                if diagonal:
                    a = jnp.where(strict, a, 0.0)
                a_blocks[kb] = a.astype(BF16)
                c = c + cs[:, block_k:]
            a = jnp.concatenate(a_blocks, axis=1)
            acc = acc + _dot(a, v_ref[pl.ds(s0, block_q), cols[hh]])
            out.append((c, acc))
        return out

    zeros = jnp.zeros((block_q, block_k), F32)
    first = chunk(pl.multiple_of(qi * block_q, block_q), True, [(zeros, zeros)] * heads_per_step)
    for hh in heads:
        c_ref[hh] = first[hh][0]
        acc_ref[hh] = first[hh][1]

    def body(it, carry):
        s0 = pl.multiple_of((qi - 1 - it) * block_q, block_q)
        new = chunk(s0, False, [(c_ref[hh], acc_ref[hh]) for hh in heads])
        for hh in heads:
            c_ref[hh] = new[hh][0]
            acc_ref[hh] = new[hh][1]
        return carry

    lax.fori_loop(0, qi, body, 0)
    for hh in heads:
        o_ref[:, cols[hh]] = acc_ref[hh].astype(o_ref.dtype)


def _sba_core(qkv, batch, seq, block_q=512, block_k=128, heads_per_step=2):
    assert block_k == HEAD_DIM
    m = qkv.shape[0]
    bq = min(block_q, seq)
    hb = heads_per_step
    w = hb * HEAD_DIM
    sec = D_MODEL // w
    qt = seq // bq
    jj = np.arange(block_k)
    uo = np.concatenate([(jj[:, None] > jj[None, :]), np.ones((block_k, block_k), bool)], axis=1)
    uo = jnp.asarray(uo.astype(np.float32), BF16)
    kern = functools.partial(_sba_kernel, block_q=bq, block_k=block_k, heads_per_step=hb)
    return pl.pallas_call(
        kern,
        grid=(batch, N_HEADS // hb, qt),
        in_specs=[
            pl.BlockSpec((bq, w), lambda b, h, i: (b * qt + i, h)),
            pl.BlockSpec((seq, w), lambda b, h, i: (b, sec + h)),
            pl.BlockSpec((seq, w), lambda b, h, i: (b, 2 * sec + h)),
            pl.BlockSpec(uo.shape, lambda b, h, i: (0, 0)),
        ],
        out_specs=pl.BlockSpec((bq, w), lambda b, h, i: (b * qt + i, h)),
        out_shape=jax.ShapeDtypeStruct((m, D_MODEL), BF16),
        scratch_shapes=[pltpu.VMEM((hb, bq, block_k), F32), pltpu.VMEM((hb, bq, HEAD_DIM), F32)],
        compiler_params=_cparams(3),
        name="sba_core",
    )(qkv, qkv, qkv, uo)


def _pool_kernel(h_ref, g0_ref, pw_ref, ps_ref, g1_ref, o_ref, ubuf_ref, y_ref,
                 *, tm, tiles_per_seq):
    ti = pl.program_id(0) % tiles_per_seq
    hist = POOL_HIST

    @pl.when(ti == 0)
    def _():
        ubuf_ref[0:hist, :] = jnp.zeros((hist, D_MODEL), F32)

    h = h_ref[...]
    ubuf_ref[hist:hist + tm, :] = _rms(h, g0_ref[...])
    pos = ti * tm + lax.broadcasted_iota(jnp.int32, (tm, POOL_GROUP), 0) + 1
    for gi, win in enumerate(POOL_WINDOWS):
        cols = slice(gi * POOL_GROUP, (gi + 1) * POOL_GROUP)
        u = ubuf_ref[hist:hist + tm, cols]
        ws = u
        for dlt in range(1, win):
            ws = ws + ubuf_ref[hist - dlt:hist - dlt + tm, cols]
        p = ws / jnp.minimum(pos, win).astype(F32) - u
        y_ref[:, cols] = _dot(p.astype(BF16), pw_ref[gi]) * ps_ref[:, cols]
    o_ref[...] = h + _rms(y_ref[...], g1_ref[...])
    ubuf_ref[0:hist, :] = ubuf_ref[tm:tm + hist, :]


def _pool_layer(h, g0, pool_w, pool_scale, g1, seq, tm=256):
    m, d = h.shape
    tm = min(tm, seq)
    kern = functools.partial(_pool_kernel, tm=tm, tiles_per_seq=seq // tm)
    row = pl.BlockSpec((1, d), lambda i: (0, 0))
    return pl.pallas_call(
        kern,
        grid=(m // tm,),
        in_specs=[
            pl.BlockSpec((tm, d), lambda i: (i, 0)),
            row,
            pl.BlockSpec(pool_w.shape, lambda i: (0, 0, 0)),
            row,
            row,
        ],
        out_specs=pl.BlockSpec((tm, d), lambda i: (i, 0)),
        out_shape=jax.ShapeDtypeStruct((m, d), F32),
        scratch_shapes=[pltpu.VMEM((tm + POOL_HIST, d), F32), pltpu.VMEM((tm, d), F32)],
        compiler_params=_cparams(1),
        name="pool_layer",
    )(h, g0, pool_w, pool_scale, g1)


def _ffn_kernel(h_ref, g2_ref, wg_ref, wv_ref, cwg_ref, cwv_ref, cbg_ref, cbv_ref, wd_ref, g3_ref,
                o_ref, xn_ref, acc_ref, ug_ref, uv_ref, carry_ref, *, tm, tf, tiles_per_seq):
    i = pl.program_id(0)
    j = pl.program_id(1)
    hist = CONV_HIST
    first = (i % tiles_per_seq) == 0

    @pl.when(j == 0)
    def _():
        xn_ref[...] = _rms(h_ref[...], g2_ref[...]).astype(BF16)
        acc_ref[...] = jnp.zeros_like(acc_ref)

    def conv(u_ref, w_ref, cw_ref, cb_ref, lane0):
        u = _dot(xn_ref[...], w_ref[...])
        prev = carry_ref[j, :, lane0:lane0 + tf]
        u_ref[0:hist, :] = jnp.where(first, 0.0, prev)
        u_ref[hist:hist + tm, :] = u
        carry_ref[j, :, lane0:lane0 + tf] = u[tm - hist:tm, :]
        return (cw_ref[0:1, :] * u_ref[hist - 2:hist - 2 + tm, :]
                + cw_ref[1:2, :] * u_ref[hist - 1:hist - 1 + tm, :]
                + cw_ref[2:3, :] * u + cb_ref[...])

    gate = conv(ug_ref, wg_ref, cwg_ref, cbg_ref, 0)
    val = conv(uv_ref, wv_ref, cwv_ref, cbv_ref, tf)
    act = (_silu(gate) * val).astype(BF16)
    acc_ref[...] += _dot(act, wd_ref[...])

    @pl.when(j == pl.num_programs(1) - 1)
    def _():
        o_ref[...] = h_ref[...] + _rms(acc_ref[...], g3_ref[...])


def _ffn_layer(h, g2, w_up, conv_w, conv_b, w_down, g3, seq, tm=512, tf=512):
    m, d = h.shape
    tm = min(tm, seq)
    nf = D_FF // tf
    kern = functools.partial(_ffn_kernel, tm=tm, tf=tf, tiles_per_seq=seq // tm)
    row = pl.BlockSpec((1, d), lambda i, j: (0, 0))
    return pl.pallas_call(
        kern,
        grid=(m // tm, nf),
        in_specs=[
            pl.BlockSpec((tm, d), lambda i, j: (i, 0)),
            row,
            pl.BlockSpec((d, tf), lambda i, j: (0, j)),
            pl.BlockSpec((d, tf), lambda i, j: (0, nf + j)),
            pl.BlockSpec((3, tf), lambda i, j: (0, j)),
            pl.BlockSpec((3, tf), lambda i, j: (0, nf + j)),
            pl.BlockSpec((1, tf), lambda i, j: (0, j)),
            pl.BlockSpec((1, tf), lambda i, j: (0, nf + j)),
            pl.BlockSpec((tf, d), lambda i, j: (j, 0)),
            row,
        ],
        out_specs=pl.BlockSpec((tm, d), lambda i, j: (i, 0)),
        out_shape=jax.ShapeDtypeStruct((m, d), F32),
        scratch_shapes=[
            pltpu.VMEM((tm, d), BF16),
            pltpu.VMEM((tm, d), F32),
            pltpu.VMEM((tm + CONV_HIST, tf), F32),
            pltpu.VMEM((tm + CONV_HIST, tf), F32),
            pltpu.VMEM((nf, CONV_HIST, 2 * tf), F32),
        ],
        compiler_params=_cparams(2),
        name="ffn_layer",
    )(h, g2, w_up, w_up, conv_w, conv_w, conv_b, conv_b, w_down, g3)


def kernel(x, norm_g, hgrn_lb_logits, hgrn_w_in, hgrn_onorm_g, hgrn_w_out, sba_w_qkv, sba_w_out,
           pool_w, pool_scale, ffn_w_up, ffn_conv_w, ffn_conv_b, ffn_w_down):
    batch, seq, d = x.shape
    depth = norm_g.shape[0]
    lb_soft = jax.nn.softmax(hgrn_lb_logits.astype(F32), axis=0)
    lower = jnp.concatenate([jnp.zeros_like(lb_soft[:1]), jnp.cumsum(lb_soft[1:], axis=0)], axis=0)
    log_lb = jnp.log(lower)
    log1m_lb = jnp.log1p(-lower)

    h = x.reshape(batch * seq, d)
    for i in range(depth):
        kind, j = i % 3, i // 3
        g = norm_g[i].reshape(4, 1, d)
        if kind == 0:
            proj = _norm_matmul(h, g[0], hgrn_w_in[j].astype(BF16), F32)
            mix = _hgrn_core(proj, log_lb[i:i + 1], log1m_lb[i:i + 1], hgrn_onorm_g[j][None, :], batch, seq)
            h = _proj_residual(mix, hgrn_w_out[j].astype(BF16), g[1], h)
        elif kind == 1:
            qkv = _norm_matmul(h, g[0], sba_w_qkv[j].astype(BF16), BF16,
                               scaled_cols=D_MODEL, scale=float(HEAD_DIM) ** -0.5 * LOG2_E)
            mix = _sba_core(qkv, batch, seq)
            h = _proj_residual(mix, sba_w_out[j].astype(BF16), g[1], h)
        else:
            h = _pool_layer(h, g[0], pool_w[j].astype(BF16), pool_scale[j][None, :], g[1], seq)
        h = _ffn_layer(h, g[2], ffn_w_up[i].astype(BF16), ffn_conv_w[i], ffn_conv_b[i][None, :],
                       ffn_w_down[i].astype(BF16), g[3], seq)
    return h.reshape(batch, seq, d)
```

```python
import functools

import numpy as np
import jax
import jax.numpy as jnp
from jax import lax
from jax.experimental import pallas as pl
from jax.experimental.pallas import tpu as pltpu

F32 = jnp.float32
BF16 = jnp.bfloat16

D_MODEL = 2048
NORM_EPS = 1e-6
LOG2_E = 1.4426950408889634
HEAD_DIM = 128
N_HEADS = D_MODEL // HEAD_DIM
HG_CHUNK = 64
HG_LEVELS = (32, 16, 8, 4, 2, 1)
POOL_WINDOWS = (2, 4, 8, 16)
POOL_GROUP = D_MODEL // len(POOL_WINDOWS)
POOL_HIST = 16
D_FF = 5632
CONV_HIST = 8
VMEM_LIMIT_BYTES = 56 * 1024 * 1024


def _cparams(n_axes):
    return pltpu.CompilerParams(
        dimension_semantics=("arbitrary",) * n_axes,
        vmem_limit_bytes=VMEM_LIMIT_BYTES)


def _rms(x, g):
    ms = jnp.mean(x * x, axis=-1, keepdims=True)
    return x * lax.rsqrt(ms + NORM_EPS) * g


def _silu(x):
    return x / (1.0 + jnp.exp(-x))


def _dot(a, b):
    return jnp.dot(a, b, preferred_element_type=F32)


def _dot_nt(a, b):
    return lax.dot_general(a, b, (((1,), (1,)), ((), ())), preferred_element_type=F32)


def _dot_tn(a, b):
    return lax.dot_general(a, b, (((0,), (0,)), ((), ())), preferred_element_type=F32)


def _split3_bf16(x):
    hi = x.astype(BF16)
    r = x - hi.astype(F32)
    mid = r.astype(BF16)
    lo = (r - mid.astype(F32)).astype(BF16)
    return hi, mid, lo


def _norm_matmul_kernel(x_ref, g_ref, w_ref, o_ref, xn_ref, *, scaled_blocks, scale):
    @pl.when(pl.program_id(1) == 0)
    def _():
        xn_ref[...] = _rms(x_ref[...], g_ref[...]).astype(BF16)

    y = _dot(xn_ref[...], w_ref[...])
    if scaled_blocks:
        y = y * jnp.where(pl.program_id(1) < scaled_blocks, scale, 1.0)
    o_ref[...] = y.astype(o_ref.dtype)


def _norm_matmul(x, g, w, layer, out_dtype, scaled_cols=0, scale=1.0, tm=1024, tn=1024):
    m, d = x.shape
    n = w.shape[2]
    kern = functools.partial(_norm_matmul_kernel, scaled_blocks=scaled_cols // tn, scale=scale)
    return pl.pallas_call(
        kern,
        grid=(m // tm, n // tn),
        in_specs=[
            pl.BlockSpec((tm, d), lambda i, j: (i, 0)),
            pl.BlockSpec((1, d), lambda i, j: (0, 0)),
            pl.BlockSpec((None, d, tn), lambda i, j: (layer, 0, j)),
        ],
        out_specs=pl.BlockSpec((tm, tn), lambda i, j: (i, j)),
        out_shape=jax.ShapeDtypeStruct((m, n), out_dtype),
        scratch_shapes=[pltpu.VMEM((tm, d), BF16)],
        compiler_params=_cparams(2),
        name="norm_matmul",
    )(x, g, w)


def _proj_residual_kernel(a_ref, w_ref, g_ref, h_ref, o_ref):
    y = _dot(a_ref[...], w_ref[...])
    o_ref[...] = h_ref[...] + _rms(y, g_ref[...])


def _proj_residual(a, w, layer, g, h, tm=512):
    m, k = a.shape
    d = w.shape[2]
    return pl.pallas_call(
        _proj_residual_kernel,
        grid=(m // tm,),
        in_specs=[
            pl.BlockSpec((tm, k), lambda i: (i, 0)),
            pl.BlockSpec((None, k, d), lambda i: (layer, 0, 0)),
            pl.BlockSpec((1, d), lambda i: (0, 0)),
            pl.BlockSpec((tm, d), lambda i: (i, 0)),
        ],
        out_specs=pl.BlockSpec((tm, d), lambda i: (i, 0)),
        out_shape=jax.ShapeDtypeStruct((m, d), F32),
        compiler_params=_cparams(1),
        name="proj_residual",
    )(a, w, g, h)


def _hgrn_tables():
    c = HG_CHUNK
    t = np.arange(c)[:, None]
    j = np.arange(c)[None, :]
    masks = [(t == j)]
    for m in HG_LEVELS:
        second = (t % (2 * m)) >= m
        masks.append(((t // (2 * m)) == (j // (2 * m))) & second & ((j % (2 * m)) < m))
    a = (j <= t).astype(np.float32)
    a = np.concatenate([a, a, a], axis=1)
    msk = np.stack(masks, axis=0).astype(np.float32)
    return a, msk


def _level_refs(b_ref, m):
    sub = lax.broadcasted_iota(jnp.int32, (8, HEAD_DIM), 0)

    def row(i):
        return jnp.broadcast_to(b_ref[i:i + 1, :], (8, HEAD_DIM))

    groups = []
    for r in range(HG_CHUNK // 8):
        if m >= 4:
            groups.append(row((8 * r // (2 * m)) * (2 * m) + m - 1))
        elif m == 2:
            groups.append(jnp.where(sub < 4, row(8 * r + 1), row(8 * r + 5)))
        else:
            groups.append(jnp.where(sub < 2, row(8 * r), jnp.where(sub < 4, row(8 * r + 2),
                                    jnp.where(sub < 6, row(8 * r + 4), row(8 * r + 6)))))
    return jnp.concatenate(groups, axis=0)


def _hgrn_kernel(q_ref, f_ref, i_ref, g_ref, al_ref, cl_ref, on_ref, a_ref, m_ref,
                 o_ref, st_ref, b_ref, *, rows_per_step, heads_per_step):
    c = HG_CHUNK

    @pl.when(pl.program_id(2) == 0)
    def _():
        st_ref[...] = jnp.zeros_like(st_ref)

    def chunk(ci, carry):
        rows = pl.ds(pl.multiple_of(ci * c, c), c)
        heads = range(heads_per_step)
        cols = [slice(hh * HEAD_DIM, (hh + 1) * HEAD_DIM) for hh in heads]
        loaded = [(f_ref[rows, cs], q_ref[rows, cs], i_ref[rows, cs], g_ref[rows, cs], st_ref[hh])
                  for hh, cs in zip(heads, cols)]
        kks, bs = [], []
        for hh in heads:
            fp = loaded[hh][0]
            ls = jnp.minimum(fp, 0.0) - jnp.log1p(jnp.exp(-jnp.abs(fp)))
            x2 = cl_ref[:, cols[hh]] + ls
            al = al_ref[:, cols[hh]]
            logf = jnp.maximum(al, x2) + jnp.log1p(jnp.exp(-jnp.abs(al - x2)))
            kks.append(1.0 - jnp.exp(logf))
            bs.append(_dot(a_ref[...], jnp.concatenate(_split3_bf16(logf), axis=0)))
        for hh in heads:
            b_ref[hh] = bs[hh]
        qs, ps = [], []
        for hh in heads:
            q = _silu(loaded[hh][1])
            kk = kks[hh]
            p = _dot_nt(q.astype(BF16), kk.astype(BF16)) * m_ref[0]
            for l, m in enumerate(HG_LEVELS):
                d = bs[hh] - _level_refs(b_ref.at[hh], m)
                el = jnp.exp(pltpu.bitcast(pltpu.bitcast(d, jnp.int32) | jnp.int32(-2 ** 31), F32))
                p = p + _dot_nt((q * el).astype(BF16), (kk * el).astype(BF16)) * m_ref[l + 1]
            qs.append(q)
            ps.append(p)
        results = []
        for hh in heads:
            b, q, st = bs[hh], qs[hh], loaded[hh][4]
            v = loaded[hh][2].astype(BF16)
            b_last = jnp.broadcast_to(b_ref[hh, c - 1:c, :], (c, HEAD_DIM))
            o = _dot(ps[hh].astype(BF16), v) + _dot_nt((q * jnp.exp(b)).astype(BF16), st.astype(BF16))
            kdec = (kks[hh] * jnp.exp(b_last - b)).astype(BF16)
            st_new = st * jnp.exp(b_last[0:1]) + _dot_tn(v, kdec)
            results.append((o, st_new))
        for hh in heads:
            o = _rms(results[hh][0], on_ref[:, cols[hh]]) * _silu(loaded[hh][3])
            results[hh] = (o.astype(o_ref.dtype), results[hh][1])
        for hh in heads:
            o_ref[rows, cols[hh]] = results[hh][0]
            st_ref[hh] = results[hh][1]
        return carry

    lax.fori_loop(0, rows_per_step // c, chunk, 0)


def _hgrn_core(proj, log_lb, log1m_lb, onorm_g, batch, seq, rows_per_step=512, heads_per_step=8):
    m = proj.shape[0]
    tr = min(rows_per_step, seq)
    hb = heads_per_step
    w = hb * HEAD_DIM
    sec = D_MODEL // w
    rt = seq // tr
    a_np, m_np = _hgrn_tables()
    a_tab = jnp.asarray(a_np, BF16)
    m_tab = jnp.asarray(m_np, F32)

    def sect(s):
        return pl.BlockSpec((tr, w), lambda b, h, r, s=s: (b * rt + r, s * sec + h))

    vec = pl.BlockSpec((1, w), lambda b, h, r: (0, h))
    kern = functools.partial(_hgrn_kernel, rows_per_step=tr, heads_per_step=hb)
    return pl.pallas_call(
        kern,
        grid=(batch, N_HEADS // hb, rt),
        in_specs=[sect(0), sect(1), sect(2), sect(3), vec, vec, vec,
                  pl.BlockSpec(a_np.shape, lambda b, h, r: (0, 0)),
                  pl.BlockSpec(m_np.shape, lambda b, h, r: (0, 0, 0))],
        out_specs=pl.BlockSpec((tr, w), lambda b, h, r: (b * rt + r, h)),
        out_shape=jax.ShapeDtypeStruct((m, D_MODEL), BF16),
        scratch_shapes=[pltpu.VMEM((hb, HEAD_DIM, HEAD_DIM), F32), pltpu.VMEM((hb, HG_CHUNK, HEAD_DIM), F32)],
        compiler_params=_cparams(3),
        name="hgrn_core",
    )(proj, proj, proj, proj, log_lb, log1m_lb, onorm_g, a_tab, m_tab)


def _sba_kernel(q_ref, k_ref, v_ref, uo_ref, o_ref, c_ref, acc_ref, *, block_q, block_k, heads_per_step):
    qi = pl.program_id(2)
    heads = range(heads_per_step)
    cols = [slice(hh * HEAD_DIM, (hh + 1) * HEAD_DIM) for hh in heads]
    nkb = block_q // block_k

    def chunk(s0, diagonal, state):
        if diagonal:
            row = lax.broadcasted_iota(jnp.int32, (block_q, block_k), 0)
            lane = lax.broadcasted_iota(jnp.int32, (block_q, block_k), 1)
        zs = [_dot_nt(q_ref[:, cols[hh]], k_ref[pl.ds(s0, block_q), cols[hh]]) for hh in heads]
        carries, probs = [], []
        for hh in heads:
            c = state[hh][0]
            a_blocks = [None] * nkb
            for kb in reversed(range(nkb)):
                zb = zs[hh][:, kb * block_k:(kb + 1) * block_k]
                nz = pltpu.bitcast(pltpu.bitcast(zb, jnp.int32) | jnp.int32(-2 ** 31), F32)
                lg = jnp.log2(1.0 + jnp.exp2(nz))
                sp = jnp.maximum(zb, 0.0) + lg
                ls = jnp.minimum(zb, 0.0) - lg
                if diagonal:
                    strict = (kb * block_k + lane) < row
                    sp = jnp.where(strict, sp, 0.0)
                cs = _dot(sp.astype(BF16), uo_ref[...])
                a = jnp.exp2(ls - cs[:, :block_k] - c)
                if diagonal:
                    a = jnp.where(strict, a, 0.0)
                a_blocks[kb] = a.astype(BF16)
                c = c + cs[:, block_k:]
            carries.append(c)
            probs.append(jnp.concatenate(a_blocks, axis=1))
        return [(carries[hh], state[hh][1] + _dot(probs[hh], v_ref[pl.ds(s0, block_q), cols[hh]]))
                for hh in heads]

    zeros = jnp.zeros((block_q, block_k), F32)
    first = chunk(pl.multiple_of(qi * block_q, block_q), True, [(zeros, zeros)] * heads_per_step)
    for hh in heads:
        c_ref[hh] = first[hh][0]
        acc_ref[hh] = first[hh][1]

    def body(it, carry):
        s0 = pl.multiple_of((qi - 1 - it) * block_q, block_q)
        new = chunk(s0, False, [(c_ref[hh], acc_ref[hh]) for hh in heads])
        for hh in heads:
            c_ref[hh] = new[hh][0]
            acc_ref[hh] = new[hh][1]
        return carry

    lax.fori_loop(0, qi, body, 0)
    for hh in heads:
        o_ref[:, cols[hh]] = acc_ref[hh].astype(o_ref.dtype)


def _sba_core(qkv, batch, seq, block_q=512, block_k=128, heads_per_step=2):
    assert block_k == HEAD_DIM
    m = qkv.shape[0]
    bq = min(block_q, seq)
    hb = heads_per_step
    w = hb * HEAD_DIM
    sec = D_MODEL // w
    qt = seq // bq
    jj = np.arange(block_k)
    uo = np.concatenate([(jj[:, None] > jj[None, :]), np.ones((block_k, block_k), bool)], axis=1)
    uo = jnp.asarray(uo.astype(np.float32), BF16)
    kern = functools.partial(_sba_kernel, block_q=bq, block_k=block_k, heads_per_step=hb)
    return pl.pallas_call(
        kern,
        grid=(batch, N_HEADS // hb, qt),
        in_specs=[
            pl.BlockSpec((bq, w), lambda b, h, i: (b * qt + i, h)),
            pl.BlockSpec((seq, w), lambda b, h, i: (b, sec + h)),
            pl.BlockSpec((seq, w), lambda b, h, i: (b, 2 * sec + h)),
            pl.BlockSpec(uo.shape, lambda b, h, i: (0, 0)),
        ],
        out_specs=pl.BlockSpec((bq, w), lambda b, h, i: (b * qt + i, h)),
        out_shape=jax.ShapeDtypeStruct((m, D_MODEL), BF16),
        scratch_shapes=[pltpu.VMEM((hb, bq, block_k), F32), pltpu.VMEM((hb, bq, HEAD_DIM), F32)],
        compiler_params=_cparams(3),
        name="sba_core",
    )(qkv, qkv, qkv, uo)


def _pool_kernel(h_ref, g0_ref, pw_ref, ps_ref, g1_ref, o_ref, ubuf_ref, y_ref,
                 *, tm, tiles_per_seq):
    ti = pl.program_id(0) % tiles_per_seq
    hist = POOL_HIST

    @pl.when(ti == 0)
    def _():
        ubuf_ref[0:hist, :] = jnp.zeros((hist, D_MODEL), F32)

    h = h_ref[...]
    ubuf_ref[hist:hist + tm, :] = _rms(h, g0_ref[...])
    pos = ti * tm + lax.broadcasted_iota(jnp.int32, (tm, POOL_GROUP), 0) + 1
    for gi, win in enumerate(POOL_WINDOWS):
        cols = slice(gi * POOL_GROUP, (gi + 1) * POOL_GROUP)
        u = ubuf_ref[hist:hist + tm, cols]
        ws = u
        for dlt in range(1, win):
            ws = ws + ubuf_ref[hist - dlt:hist - dlt + tm, cols]
        p = ws / jnp.minimum(pos, win).astype(F32) - u
        y_ref[:, cols] = _dot(p.astype(BF16), pw_ref[gi]) * ps_ref[:, cols]
    o_ref[...] = h + _rms(y_ref[...], g1_ref[...])
    ubuf_ref[0:hist, :] = ubuf_ref[tm:tm + hist, :]


def _pool_layer(h, g0, pool_w, layer, pool_scale, g1, seq, tm=256):
    m, d = h.shape
    tm = min(tm, seq)
    kern = functools.partial(_pool_kernel, tm=tm, tiles_per_seq=seq // tm)
    row = pl.BlockSpec((1, d), lambda i: (0, 0))
    return pl.pallas_call(
        kern,
        grid=(m // tm,),
        in_specs=[
            pl.BlockSpec((tm, d), lambda i: (i, 0)),
            row,
            pl.BlockSpec((None,) + pool_w.shape[1:], lambda i: (layer, 0, 0, 0)),
            row,
            row,
        ],
        out_specs=pl.BlockSpec((tm, d), lambda i: (i, 0)),
        out_shape=jax.ShapeDtypeStruct((m, d), F32),
        scratch_shapes=[pltpu.VMEM((tm + POOL_HIST, d), F32), pltpu.VMEM((tm, d), F32)],
        compiler_params=_cparams(1),
        name="pool_layer",
    )(h, g0, pool_w, pool_scale, g1)


def _ffn_kernel(h_ref, g2_ref, wg_ref, wv_ref, cwg_ref, cwv_ref, cbg_ref, cbv_ref, wd_ref, g3_ref,
                o_ref, xn_ref, acc_ref, ug_ref, uv_ref, carry_ref, act_ref, act_next_ref,
                *, tm, tf, nf, tiles_per_seq):
    i = pl.program_id(0)
    j = pl.program_id(1)
    hist = CONV_HIST
    first = (i % tiles_per_seq) == 0

    def conv(u_ref, w_ref, cw_ref, cb_ref, lane0):
        u_ref[hist:hist + tm, :] = _dot(xn_ref[...], w_ref[...])
        u_ref[0:hist, :] = jnp.where(first, 0.0, carry_ref[j, :, lane0:lane0 + tf])
        carry_ref[j, :, lane0:lane0 + tf] = u_ref[tm:tm + hist, :]
        return (cw_ref[0:1, :] * u_ref[hist - 2:hist - 2 + tm, :]
                + cw_ref[1:2, :] * u_ref[hist - 1:hist - 1 + tm, :]
                + cw_ref[2:3, :] * u_ref[hist:hist + tm, :] + cb_ref[...])

    def up_chunk(dst_ref):
        gate = conv(ug_ref, wg_ref, cwg_ref, cbg_ref, 0)
        val = conv(uv_ref, wv_ref, cwv_ref, cbv_ref, tf)
        dst_ref[...] = (_silu(gate) * val).astype(BF16)

    @pl.when(j == 0)
    def _():
        xn_ref[...] = _rms(h_ref[...], g2_ref[...]).astype(BF16)
        acc_ref[...] = jnp.zeros_like(acc_ref)
        up_chunk(act_ref)

    act_refs = (act_ref, act_next_ref)
    steady = jnp.logical_and(j > 0, j < nf)
    for parity in range(2):
        @pl.when(jnp.logical_and(steady, j % 2 == parity))
        def _(parity=parity):
            up_chunk(act_refs[parity])
            acc_ref[...] += _dot(act_refs[1 - parity][...], wd_ref[...])

    @pl.when(j == nf)
    def _():
        y = acc_ref[...] + _dot(act_refs[(nf - 1) % 2][...], wd_ref[...])
        o_ref[...] = h_ref[...] + _rms(y, g3_ref[...])


def _ffn_layer(h, g2, w_up, conv_w, conv_b, w_down, layer, g3, seq, tm=512, tf=512):
    m, d = h.shape
    tm = min(tm, seq)
    nf = D_FF // tf
    kern = functools.partial(_ffn_kernel, tm=tm, tf=tf, nf=nf, tiles_per_seq=seq // tm)
    row = pl.BlockSpec((1, d), lambda i, j: (0, 0))

    def up_side(rows, half):
        return pl.BlockSpec((None, rows, tf), lambda i, j: (layer, 0, half * nf + jnp.minimum(j, nf - 1)))

    return pl.pallas_call(
        kern,
        grid=(m // tm, nf + 1),
        in_specs=[
            pl.BlockSpec((tm, d), lambda i, j: (i, 0)),
            row,
            up_side(d, 0), up_side(d, 1),
            up_side(3, 0), up_side(3, 1),
            up_side(1, 0), up_side(1, 1),
            pl.BlockSpec((None, tf, d), lambda i, j: (layer, jnp.maximum(j - 1, 0), 0)),
            row,
        ],
        out_specs=pl.BlockSpec((tm, d), lambda i, j: (i, 0)),
        out_shape=jax.ShapeDtypeStruct((m, d), F32),
        scratch_shapes=[
            pltpu.VMEM((tm, d), BF16),
            pltpu.VMEM((tm, d), F32),
            pltpu.VMEM((tm + CONV_HIST, tf), F32),
            pltpu.VMEM((tm + CONV_HIST, tf), F32),
            pltpu.VMEM((nf, CONV_HIST, 2 * tf), F32),
            pltpu.VMEM((tm, tf), BF16),
            pltpu.VMEM((tm, tf), BF16),
        ],
        compiler_params=_cparams(2),
        name="ffn_layer",
    )(h, g2, w_up, w_up, conv_w, conv_w, conv_b, conv_b, w_down, g3)


def kernel(x, norm_g, hgrn_lb_logits, hgrn_w_in, hgrn_onorm_g, hgrn_w_out, sba_w_qkv, sba_w_out,
           pool_w, pool_scale, ffn_w_up, ffn_conv_w, ffn_conv_b, ffn_w_down):
    batch, seq, d = x.shape
    depth = norm_g.shape[0]
    lb_soft = jax.nn.softmax(hgrn_lb_logits.astype(F32), axis=0)
    lower = jnp.concatenate([jnp.zeros_like(lb_soft[:1]), jnp.cumsum(lb_soft[1:], axis=0)], axis=0)
    log_lb = jnp.log(lower)
    log1m_lb = jnp.log1p(-lower)

    hgrn_w_in, hgrn_w_out, sba_w_qkv, sba_w_out, pool_w, ffn_w_up, ffn_w_down = (
        w.astype(BF16) for w in (hgrn_w_in, hgrn_w_out, sba_w_qkv, sba_w_out, pool_w, ffn_w_up, ffn_w_down))
    ffn_conv_b = ffn_conv_b[:, None, :]

    h = x.reshape(batch * seq, d)
    for i in range(depth):
        kind, j = i % 3, i // 3
        g = norm_g[i].reshape(4, 1, d)
        if kind == 0:
            proj = _norm_matmul(h, g[0], hgrn_w_in, j, F32)
            mix = _hgrn_core(proj, log_lb[i:i + 1], log1m_lb[i:i + 1], hgrn_onorm_g[j][None, :], batch, seq)
            h = _proj_residual(mix, hgrn_w_out, j, g[1], h)
        elif kind == 1:
            qkv = _norm_matmul(h, g[0], sba_w_qkv, j, BF16,
                               scaled_cols=D_MODEL, scale=float(HEAD_DIM) ** -0.5 * LOG2_E)
            mix = _sba_core(qkv, batch, seq)
            h = _proj_residual(mix, sba_w_out, j, g[1], h)
        else:
            h = _pool_layer(h, g[0], pool_w, j, pool_scale[j][None, :], g[1], seq)
        h = _ffn_layer(h, g[2], ffn_w_up, ffn_conv_w, ffn_conv_b, ffn_w_down, i, g[3], seq)
    return h.reshape(batch, seq, d)
```

```python
import functools

import numpy as np
import jax
import jax.numpy as jnp
from jax import lax
from jax.experimental import pallas as pl
from jax.experimental.pallas import tpu as pltpu

F32 = jnp.float32
BF16 = jnp.bfloat16

D_MODEL = 2048
NORM_EPS = 1e-6
LOG2_E = 1.4426950408889634
HEAD_DIM = 128
N_HEADS = D_MODEL // HEAD_DIM
HG_CHUNK = 64
HG_LEVELS = (32, 16, 8, 4, 2, 1)
POOL_WINDOWS = (2, 4, 8, 16)
POOL_GROUP = D_MODEL // len(POOL_WINDOWS)
POOL_HIST = 16
D_FF = 5632
FFN_ROW_PARTS = 2
CONV_HIST = 8
VMEM_LIMIT_BYTES = 56 * 1024 * 1024


def _cparams(n_axes):
    return pltpu.CompilerParams(
        dimension_semantics=("arbitrary",) * n_axes,
        vmem_limit_bytes=VMEM_LIMIT_BYTES)


def _rms(x, g):
    ms = jnp.mean(x * x, axis=-1, keepdims=True)
    return x * lax.rsqrt(ms + NORM_EPS) * g


def _silu(x):
    return x / (1.0 + jnp.exp(-x))


def _dot(a, b):
    return jnp.dot(a, b, preferred_element_type=F32)


def _dot_nt(a, b):
    return lax.dot_general(a, b, (((1,), (1,)), ((), ())), preferred_element_type=F32)


def _dot_tn(a, b):
    return lax.dot_general(a, b, (((0,), (0,)), ((), ())), preferred_element_type=F32)


def _split3_bf16(x):
    hi = x.astype(BF16)
    r = x - hi.astype(F32)
    mid = r.astype(BF16)
    lo = (r - mid.astype(F32)).astype(BF16)
    return hi, mid, lo


def _norm_matmul_kernel(x_ref, g_ref, w_ref, o_ref, xn_ref, *, scaled_blocks, scale):
    @pl.when(pl.program_id(1) == 0)
    def _():
        xn_ref[...] = _rms(x_ref[...], g_ref[...]).astype(BF16)

    y = _dot(xn_ref[...], w_ref[...])
    if scaled_blocks:
        y = y * jnp.where(pl.program_id(1) < scaled_blocks, scale, 1.0)
    o_ref[...] = y.astype(o_ref.dtype)


def _norm_matmul(x, g, w, layer, out_dtype, scaled_cols=0, scale=1.0, tm=1024, tn=1024):
    m, d = x.shape
    n = w.shape[2]
    kern = functools.partial(_norm_matmul_kernel, scaled_blocks=scaled_cols // tn, scale=scale)
    return pl.pallas_call(
        kern,
        grid=(m // tm, n // tn),
        in_specs=[
            pl.BlockSpec((tm, d), lambda i, j: (i, 0)),
            pl.BlockSpec((1, d), lambda i, j: (0, 0)),
            pl.BlockSpec((None, d, tn), lambda i, j: (layer, 0, j)),
        ],
        out_specs=pl.BlockSpec((tm, tn), lambda i, j: (i, j)),
        out_shape=jax.ShapeDtypeStruct((m, n), out_dtype),
        scratch_shapes=[pltpu.VMEM((tm, d), BF16)],
        compiler_params=_cparams(2),
        name="norm_matmul",
    )(x, g, w)


def _proj_residual_kernel(a_ref, w_ref, g_ref, h_ref, o_ref):
    rp = a_ref.shape[0] // 2
    ys = [_dot(a_ref[p * rp:(p + 1) * rp, :], w_ref[...]) for p in range(2)]
    for p in range(2):
        rows = slice(p * rp, (p + 1) * rp)
        o_ref[rows, :] = h_ref[rows, :] + _rms(ys[p], g_ref[...])


def _proj_residual(a, w, layer, g, h, tm=512):
    m, k = a.shape
    d = w.shape[2]
    return pl.pallas_call(
        _proj_residual_kernel,
        grid=(m // tm,),
        in_specs=[
            pl.BlockSpec((tm, k), lambda i: (i, 0)),
            pl.BlockSpec((None, k, d), lambda i: (layer, 0, 0)),
            pl.BlockSpec((1, d), lambda i: (0, 0)),
            pl.BlockSpec((tm, d), lambda i: (i, 0)),
        ],
        out_specs=pl.BlockSpec((tm, d), lambda i: (i, 0)),
        out_shape=jax.ShapeDtypeStruct((m, d), F32),
        compiler_params=_cparams(1),
        name="proj_residual",
    )(a, w, g, h)


def _hgrn_tables():
    c = HG_CHUNK
    t = np.arange(c)[:, None]
    j = np.arange(c)[None, :]
    masks = [(t == j)]
    for m in HG_LEVELS:
        second = (t % (2 * m)) >= m
        masks.append(((t // (2 * m)) == (j // (2 * m))) & second & ((j % (2 * m)) < m))
    a = (j <= t).astype(np.float32)
    a = np.concatenate([a, a, a], axis=1)
    msk = np.stack(masks, axis=0).astype(np.float32)
    return a, msk


def _level_refs(b_ref, m):
    sub = lax.broadcasted_iota(jnp.int32, (8, HEAD_DIM), 0)

    def row(i):
        return jnp.broadcast_to(b_ref[i:i + 1, :], (8, HEAD_DIM))

    groups = []
    for r in range(HG_CHUNK // 8):
        if m >= 4:
            groups.append(row((8 * r // (2 * m)) * (2 * m) + m - 1))
        elif m == 2:
            groups.append(jnp.where(sub < 4, row(8 * r + 1), row(8 * r + 5)))
        else:
            groups.append(jnp.where(sub < 2, row(8 * r), jnp.where(sub < 4, row(8 * r + 2),
                                    jnp.where(sub < 6, row(8 * r + 4), row(8 * r + 6)))))
    return jnp.concatenate(groups, axis=0)


def _hgrn_kernel(q_ref, f_ref, i_ref, g_ref, al_ref, cl_ref, on_ref, a_ref, m_ref,
                 o_ref, st_ref, b_ref, *, rows_per_step, heads_per_step):
    c = HG_CHUNK

    @pl.when(pl.program_id(2) == 0)
    def _():
        st_ref[...] = jnp.zeros_like(st_ref)

    def chunk(ci, carry):
        rows = pl.ds(pl.multiple_of(ci * c, c), c)
        heads = range(heads_per_step)
        cols = [slice(hh * HEAD_DIM, (hh + 1) * HEAD_DIM) for hh in heads]
        loaded = [(f_ref[rows, cs], q_ref[rows, cs], i_ref[rows, cs], g_ref[rows, cs], st_ref[hh])
                  for hh, cs in zip(heads, cols)]
        kks, bs = [], []
        for hh in heads:
            fp = loaded[hh][0]
            ls = jnp.minimum(fp, 0.0) - jnp.log1p(jnp.exp(-jnp.abs(fp)))
            x2 = cl_ref[:, cols[hh]] + ls
            al = al_ref[:, cols[hh]]
            logf = jnp.maximum(al, x2) + jnp.log1p(jnp.exp(-jnp.abs(al - x2)))
            kks.append(1.0 - jnp.exp(logf))
            bs.append(_dot(a_ref[...], jnp.concatenate(_split3_bf16(logf), axis=0)))
        for hh in heads:
            b_ref[hh] = bs[hh]
        qs, ps = [], []
        for hh in heads:
            q = _silu(loaded[hh][1])
            kk = kks[hh]
            p = _dot_nt(q.astype(BF16), kk.astype(BF16)) * m_ref[0]
            for l, m in enumerate(HG_LEVELS):
                d = bs[hh] - _level_refs(b_ref.at[hh], m)
                el = jnp.exp(pltpu.bitcast(pltpu.bitcast(d, jnp.int32) | jnp.int32(-2 ** 31), F32))
                p = p + _dot_nt((q * el).astype(BF16), (kk * el).astype(BF16)) * m_ref[l + 1]
            qs.append(q)
            ps.append(p)
        results = []
        for hh in heads:
            b, q, st = bs[hh], qs[hh], loaded[hh][4]
            v = loaded[hh][2].astype(BF16)
            b_last = jnp.broadcast_to(b_ref[hh, c - 1:c, :], (c, HEAD_DIM))
            o = _dot(ps[hh].astype(BF16), v) + _dot_nt((q * jnp.exp(b)).astype(BF16), st.astype(BF16))
            kdec = (kks[hh] * jnp.exp(b_last - b)).astype(BF16)
            st_new = st * jnp.exp(b_last[0:1]) + _dot_tn(v, kdec)
            results.append((o, st_new))
        for hh in heads:
            o = _rms(results[hh][0], on_ref[:, cols[hh]]) * _silu(loaded[hh][3])
            results[hh] = (o.astype(o_ref.dtype), results[hh][1])
        for hh in heads:
            o_ref[rows, cols[hh]] = results[hh][0]
            st_ref[hh] = results[hh][1]
        return carry

    lax.fori_loop(0, rows_per_step // c, chunk, 0)


def _hgrn_core(proj, log_lb, log1m_lb, onorm_g, batch, seq, rows_per_step=512, heads_per_step=8):
    m = proj.shape[0]
    tr = min(rows_per_step, seq)
    hb = heads_per_step
    w = hb * HEAD_DIM
    sec = D_MODEL // w
    rt = seq // tr
    a_np, m_np = _hgrn_tables()
    a_tab = jnp.asarray(a_np, BF16)
    m_tab = jnp.asarray(m_np, F32)

    def sect(s):
        return pl.BlockSpec((tr, w), lambda b, h, r, s=s: (b * rt + r, s * sec + h))

    vec = pl.BlockSpec((1, w), lambda b, h, r: (0, h))
    kern = functools.partial(_hgrn_kernel, rows_per_step=tr, heads_per_step=hb)
    return pl.pallas_call(
        kern,
        grid=(batch, N_HEADS // hb, rt),
        in_specs=[sect(0), sect(1), sect(2), sect(3), vec, vec, vec,
                  pl.BlockSpec(a_np.shape, lambda b, h, r: (0, 0)),
                  pl.BlockSpec(m_np.shape, lambda b, h, r: (0, 0, 0))],
        out_specs=pl.BlockSpec((tr, w), lambda b, h, r: (b * rt + r, h)),
        out_shape=jax.ShapeDtypeStruct((m, D_MODEL), BF16),
        scratch_shapes=[pltpu.VMEM((hb, HEAD_DIM, HEAD_DIM), F32), pltpu.VMEM((hb, HG_CHUNK, HEAD_DIM), F32)],
        compiler_params=_cparams(3),
        name="hgrn_core",
    )(proj, proj, proj, proj, log_lb, log1m_lb, onorm_g, a_tab, m_tab)


def _sba_kernel(q_ref, k_ref, v_ref, uo_ref, o_ref, c_ref, acc_ref, *, block_q, block_k, heads_per_step):
    qi = pl.program_id(2)
    heads = range(heads_per_step)
    cols = [slice(hh * HEAD_DIM, (hh + 1) * HEAD_DIM) for hh in heads]
    nkb = block_q // block_k

    def chunk(s0, diagonal, state):
        if diagonal:
            row = lax.broadcasted_iota(jnp.int32, (block_q, block_k), 0)
            lane = lax.broadcasted_iota(jnp.int32, (block_q, block_k), 1)
        zs = [_dot_nt(q_ref[:, cols[hh]], k_ref[pl.ds(s0, block_q), cols[hh]]) for hh in heads]
        carries, probs = [], []
        for hh in heads:
            c = state[hh][0]
            a_blocks = [None] * nkb
            for kb in reversed(range(nkb)):
                zb = zs[hh][:, kb * block_k:(kb + 1) * block_k]
                nz = pltpu.bitcast(pltpu.bitcast(zb, jnp.int32) | jnp.int32(-2 ** 31), F32)
                lg = jnp.log2(1.0 + jnp.exp2(nz))
                sp = jnp.maximum(zb, 0.0) + lg
                ls = jnp.minimum(zb, 0.0) - lg
                if diagonal:
                    strict = (kb * block_k + lane) < row
                    sp = jnp.where(strict, sp, 0.0)
                cs = _dot(sp.astype(BF16), uo_ref[...])
                a = jnp.exp2(ls - cs[:, :block_k] - c)
                if diagonal:
                    a = jnp.where(strict, a, 0.0)
                a_blocks[kb] = a.astype(BF16)
                c = c + cs[:, block_k:]
            carries.append(c)
            probs.append(jnp.concatenate(a_blocks, axis=1))
        return [(carries[hh], state[hh][1] + _dot(probs[hh], v_ref[pl.ds(s0, block_q), cols[hh]]))
                for hh in heads]

    zeros = jnp.zeros((block_q, block_k), F32)
    first = chunk(pl.multiple_of(qi * block_q, block_q), True, [(zeros, zeros)] * heads_per_step)
    for hh in heads:
        c_ref[hh] = first[hh][0]
        acc_ref[hh] = first[hh][1]

    def body(it, carry):
        s0 = pl.multiple_of((qi - 1 - it) * block_q, block_q)
        new = chunk(s0, False, [(c_ref[hh], acc_ref[hh]) for hh in heads])
        for hh in heads:
            c_ref[hh] = new[hh][0]
            acc_ref[hh] = new[hh][1]
        return carry

    lax.fori_loop(0, qi, body, 0)
    for hh in heads:
        o_ref[:, cols[hh]] = acc_ref[hh].astype(o_ref.dtype)


def _sba_core(qkv, batch, seq, block_q=512, block_k=128, heads_per_step=2):
    assert block_k == HEAD_DIM
    m = qkv.shape[0]
    bq = min(block_q, seq)
    hb = heads_per_step
    w = hb * HEAD_DIM
    sec = D_MODEL // w
    qt = seq // bq
    jj = np.arange(block_k)
    uo = np.concatenate([(jj[:, None] > jj[None, :]), np.ones((block_k, block_k), bool)], axis=1)
    uo = jnp.asarray(uo.astype(np.float32), BF16)
    kern = functools.partial(_sba_kernel, block_q=bq, block_k=block_k, heads_per_step=hb)
    return pl.pallas_call(
        kern,
        grid=(batch, N_HEADS // hb, qt),
        in_specs=[
            pl.BlockSpec((bq, w), lambda b, h, i: (b * qt + i, h)),
            pl.BlockSpec((seq, w), lambda b, h, i: (b, sec + h)),
            pl.BlockSpec((seq, w), lambda b, h, i: (b, 2 * sec + h)),
            pl.BlockSpec(uo.shape, lambda b, h, i: (0, 0)),
        ],
        out_specs=pl.BlockSpec((bq, w), lambda b, h, i: (b * qt + i, h)),
        out_shape=jax.ShapeDtypeStruct((m, D_MODEL), BF16),
        scratch_shapes=[pltpu.VMEM((hb, bq, block_k), F32), pltpu.VMEM((hb, bq, HEAD_DIM), F32)],
        compiler_params=_cparams(3),
        name="sba_core",
    )(qkv, qkv, qkv, uo)


def _pool_kernel(h_ref, g0_ref, pw_ref, ps_ref, g1_ref, o_ref, ubuf_ref, y_ref,
                 *, tm, tiles_per_seq):
    ti = pl.program_id(0) % tiles_per_seq
    hist = POOL_HIST

    @pl.when(ti == 0)
    def _():
        ubuf_ref[0:hist, :] = jnp.zeros((hist, D_MODEL), F32)

    h = h_ref[...]
    ubuf_ref[hist:hist + tm, :] = _rms(h, g0_ref[...])
    pos = ti * tm + lax.broadcasted_iota(jnp.int32, (tm, POOL_GROUP), 0) + 1
    for gi, win in enumerate(POOL_WINDOWS):
        cols = slice(gi * POOL_GROUP, (gi + 1) * POOL_GROUP)
        u = ubuf_ref[hist:hist + tm, cols]
        ws = u
        for dlt in range(1, win):
            ws = ws + ubuf_ref[hist - dlt:hist - dlt + tm, cols]
        p = ws / jnp.minimum(pos, win).astype(F32) - u
        y_ref[:, cols] = _dot(p.astype(BF16), pw_ref[gi]) * ps_ref[:, cols]
    o_ref[...] = h + _rms(y_ref[...], g1_ref[...])
    ubuf_ref[0:hist, :] = ubuf_ref[tm:tm + hist, :]


def _pool_layer(h, g0, pool_w, layer, pool_scale, g1, seq, tm=256):
    m, d = h.shape
    tm = min(tm, seq)
    kern = functools.partial(_pool_kernel, tm=tm, tiles_per_seq=seq // tm)
    row = pl.BlockSpec((1, d), lambda i: (0, 0))
    return pl.pallas_call(
        kern,
        grid=(m // tm,),
        in_specs=[
            pl.BlockSpec((tm, d), lambda i: (i, 0)),
            row,
            pl.BlockSpec((None,) + pool_w.shape[1:], lambda i: (layer, 0, 0, 0)),
            row,
            row,
        ],
        out_specs=pl.BlockSpec((tm, d), lambda i: (i, 0)),
        out_shape=jax.ShapeDtypeStruct((m, d), F32),
        scratch_shapes=[pltpu.VMEM((tm + POOL_HIST, d), F32), pltpu.VMEM((tm, d), F32)],
        compiler_params=_cparams(1),
        name="pool_layer",
    )(h, g0, pool_w, pool_scale, g1)


def _ffn_kernel(h_ref, g2_ref, wg_ref, wv_ref, cwg_ref, cwv_ref, cbg_ref, cbv_ref, wd_ref, g3_ref,
                o_ref, xn_ref, acc_ref, ug_ref, uv_ref, carry_ref, *, tm, tf, tiles_per_seq):
    i = pl.program_id(0)
    j = pl.program_id(1)
    hist = CONV_HIST
    first = (i % tiles_per_seq) == 0
    rp = tm // FFN_ROW_PARTS

    @pl.when(j == 0)
    def _():
        xn_ref[...] = _rms(h_ref[...], g2_ref[...]).astype(BF16)
        acc_ref[...] = jnp.zeros_like(acc_ref)

    ug_ref[0:hist, :] = jnp.where(first, 0.0, carry_ref[j, :, 0:tf])
    uv_ref[0:hist, :] = jnp.where(first, 0.0, carry_ref[j, :, tf:2 * tf])
    for p in range(FFN_ROW_PARTS):
        ug_ref[hist + p * rp:hist + (p + 1) * rp, :] = _dot(xn_ref[p * rp:(p + 1) * rp, :], wg_ref[...])
        uv_ref[hist + p * rp:hist + (p + 1) * rp, :] = _dot(xn_ref[p * rp:(p + 1) * rp, :], wv_ref[...])
    carry_ref[j, :, 0:tf] = ug_ref[tm:tm + hist, :]
    carry_ref[j, :, tf:2 * tf] = uv_ref[tm:tm + hist, :]

    def conv(u_ref, cw_ref, cb_ref, r0):
        return (cw_ref[0:1, :] * u_ref[r0 - 2:r0 - 2 + rp, :] + cw_ref[1:2, :] * u_ref[r0 - 1:r0 - 1 + rp, :]
                + cw_ref[2:3, :] * u_ref[r0:r0 + rp, :] + cb_ref[...])

    for p in range(FFN_ROW_PARTS):
        r0 = hist + p * rp
        act = (_silu(conv(ug_ref, cwg_ref, cbg_ref, r0)) * conv(uv_ref, cwv_ref, cbv_ref, r0)).astype(BF16)
        acc_ref[p * rp:(p + 1) * rp, :] += _dot(act, wd_ref[...])

    @pl.when(j == pl.num_programs(1) - 1)
    def _():
        o_ref[...] = h_ref[...] + _rms(acc_ref[...], g3_ref[...])


def _ffn_layer(h, g2, w_up, conv_w, conv_b, w_down, layer, g3, seq, tm=512, tf=512):
    m, d = h.shape
    tm = min(tm, seq)
    nf = D_FF // tf
    kern = functools.partial(_ffn_kernel, tm=tm, tf=tf, tiles_per_seq=seq // tm)
    row = pl.BlockSpec((1, d), lambda i, j: (0, 0))

    def up_side(rows, half):
        return pl.BlockSpec((None, rows, tf), lambda i, j: (layer, 0, half * nf + j))

    return pl.pallas_call(
        kern,
        grid=(m // tm, nf),
        in_specs=[
            pl.BlockSpec((tm, d), lambda i, j: (i, 0)),
            row,
            up_side(d, 0), up_side(d, 1),
            up_side(3, 0), up_side(3, 1),
            up_side(1, 0), up_side(1, 1),
            pl.BlockSpec((None, tf, d), lambda i, j: (layer, j, 0)),
            row,
        ],
        out_specs=pl.BlockSpec((tm, d), lambda i, j: (i, 0)),
        out_shape=jax.ShapeDtypeStruct((m, d), F32),
        scratch_shapes=[
            pltpu.VMEM((tm, d), BF16),
            pltpu.VMEM((tm, d), F32),
            pltpu.VMEM((tm + CONV_HIST, tf), F32),
            pltpu.VMEM((tm + CONV_HIST, tf), F32),
            pltpu.VMEM((nf, CONV_HIST, 2 * tf), F32),
        ],
        compiler_params=_cparams(2),
        name="ffn_layer",
    )(h, g2, w_up, w_up, conv_w, conv_w, conv_b, conv_b, w_down, g3)


def kernel(x, norm_g, hgrn_lb_logits, hgrn_w_in, hgrn_onorm_g, hgrn_w_out, sba_w_qkv, sba_w_out,
           pool_w, pool_scale, ffn_w_up, ffn_conv_w, ffn_conv_b, ffn_w_down):
    batch, seq, d = x.shape
    depth = norm_g.shape[0]
    lb_soft = jax.nn.softmax(hgrn_lb_logits.astype(F32), axis=0)
    lower = jnp.concatenate([jnp.zeros_like(lb_soft[:1]), jnp.cumsum(lb_soft[1:], axis=0)], axis=0)
    log_lb = jnp.log(lower)
    log1m_lb = jnp.log1p(-lower)

    hgrn_w_in, hgrn_w_out, sba_w_qkv, sba_w_out, pool_w, ffn_w_up, ffn_w_down = (
        w.astype(BF16) for w in (hgrn_w_in, hgrn_w_out, sba_w_qkv, sba_w_out, pool_w, ffn_w_up, ffn_w_down))
    ffn_conv_b = ffn_conv_b[:, None, :]

    h = x.reshape(batch * seq, d)
    for i in range(depth):
        kind, j = i % 3, i // 3
        g = norm_g[i].reshape(4, 1, d)
        if kind == 0:
            proj = _norm_matmul(h, g[0], hgrn_w_in, j, F32)
            mix = _hgrn_core(proj, log_lb[i:i + 1], log1m_lb[i:i + 1], hgrn_onorm_g[j][None, :], batch, seq)
            h = _proj_residual(mix, hgrn_w_out, j, g[1], h)
        elif kind == 1:
            qkv = _norm_matmul(h, g[0], sba_w_qkv, j, BF16,
                               scaled_cols=D_MODEL, scale=float(HEAD_DIM) ** -0.5 * LOG2_E)
            mix = _sba_core(qkv, batch, seq)
            h = _proj_residual(mix, sba_w_out, j, g[1], h)
        else:
            h = _pool_layer(h, g[0], pool_w, j, pool_scale[j][None, :], g[1], seq)
        h = _ffn_layer(h, g[2], ffn_w_up, ffn_conv_w, ffn_conv_b, ffn_w_down, i, g[3], seq)
    return h.reshape(batch, seq, d)
```

```python
import functools

import numpy as np
import jax
import jax.numpy as jnp
from jax import lax
from jax.experimental import pallas as pl
from jax.experimental.pallas import tpu as pltpu

F32 = jnp.float32
BF16 = jnp.bfloat16

D_MODEL = 2048
NORM_EPS = 1e-6
LOG2_E = 1.4426950408889634
HEAD_DIM = 128
N_HEADS = D_MODEL // HEAD_DIM
HG_CHUNK = 64
HG_LEVELS = (32, 16, 8, 4, 2, 1)
POOL_WINDOWS = (2, 4, 8, 16)
POOL_GROUP = D_MODEL // len(POOL_WINDOWS)
POOL_HIST = 16
D_FF = 5632
FFN_ROW_PARTS = 2
CONV_HIST = 8
VMEM_LIMIT_BYTES = 56 * 1024 * 1024


def _cparams(n_axes):
    return pltpu.CompilerParams(
        dimension_semantics=("arbitrary",) * n_axes,
        vmem_limit_bytes=VMEM_LIMIT_BYTES)


def _rms(x, g):
    ms = jnp.mean(x * x, axis=-1, keepdims=True)
    return x * lax.rsqrt(ms + NORM_EPS) * g


def _silu(x):
    return x / (1.0 + jnp.exp(-x))


def _dot(a, b):
    return jnp.dot(a, b, preferred_element_type=F32)


def _dot_nt(a, b):
    return lax.dot_general(a, b, (((1,), (1,)), ((), ())), preferred_element_type=F32)


def _dot_tn(a, b):
    return lax.dot_general(a, b, (((0,), (0,)), ((), ())), preferred_element_type=F32)


def _split3_bf16(x):
    hi = x.astype(BF16)
    r = x - hi.astype(F32)
    mid = r.astype(BF16)
    lo = (r - mid.astype(F32)).astype(BF16)
    return hi, mid, lo


def _norm_matmul_kernel(x_ref, g_ref, w_ref, o_ref, xn_ref, *, scaled_blocks, scale):
    @pl.when(pl.program_id(1) == 0)
    def _():
        xn_ref[...] = _rms(x_ref[...], g_ref[...]).astype(BF16)

    y = _dot(xn_ref[...], w_ref[...])
    if scaled_blocks:
        y = y * jnp.where(pl.program_id(1) < scaled_blocks, scale, 1.0)
    o_ref[...] = y.astype(o_ref.dtype)


def _norm_matmul(x, g, w, layer, out_dtype, scaled_cols=0, scale=1.0, tm=1024, tn=1024):
    m, d = x.shape
    n = w.shape[2]
    kern = functools.partial(_norm_matmul_kernel, scaled_blocks=scaled_cols // tn, scale=scale)
    return pl.pallas_call(
        kern,
        grid=(m // tm, n // tn),
        in_specs=[
            pl.BlockSpec((tm, d), lambda i, j: (i, 0)),
            pl.BlockSpec((1, d), lambda i, j: (0, 0)),
            pl.BlockSpec((None, d, tn), lambda i, j: (layer, 0, j)),
        ],
        out_specs=pl.BlockSpec((tm, tn), lambda i, j: (i, j)),
        out_shape=jax.ShapeDtypeStruct((m, n), out_dtype),
        scratch_shapes=[pltpu.VMEM((tm, d), BF16)],
        compiler_params=_cparams(2),
        name="norm_matmul",
    )(x, g, w)


def _proj_residual_kernel(a_ref, w_ref, g_ref, h_ref, o_ref):
    rp = a_ref.shape[0] // 2
    ys = [_dot(a_ref[p * rp:(p + 1) * rp, :], w_ref[...]) for p in range(2)]
    for p in range(2):
        rows = slice(p * rp, (p + 1) * rp)
        o_ref[rows, :] = h_ref[rows, :] + _rms(ys[p], g_ref[...])


def _proj_residual(a, w, layer, g, h, tm=512):
    m, k = a.shape
    d = w.shape[2]
    return pl.pallas_call(
        _proj_residual_kernel,
        grid=(m // tm,),
        in_specs=[
            pl.BlockSpec((tm, k), lambda i: (i, 0)),
            pl.BlockSpec((None, k, d), lambda i: (layer, 0, 0)),
            pl.BlockSpec((1, d), lambda i: (0, 0)),
            pl.BlockSpec((tm, d), lambda i: (i, 0)),
        ],
        out_specs=pl.BlockSpec((tm, d), lambda i: (i, 0)),
        out_shape=jax.ShapeDtypeStruct((m, d), F32),
        compiler_params=_cparams(1),
        name="proj_residual",
    )(a, w, g, h)


def _hgrn_tables():
    c = HG_CHUNK
    t = np.arange(c)[:, None]
    j = np.arange(c)[None, :]
    masks = [(t == j)]
    for m in HG_LEVELS:
        second = (t % (2 * m)) >= m
        masks.append(((t // (2 * m)) == (j // (2 * m))) & second & ((j % (2 * m)) < m))
    a = (j <= t).astype(np.float32)
    a = np.concatenate([a, a, a], axis=1)
    msk = np.stack(masks, axis=0).astype(np.float32)
    return a, msk


def _level_refs(b_ref, m):
    sub = lax.broadcasted_iota(jnp.int32, (8, HEAD_DIM), 0)

    def row(i):
        return jnp.broadcast_to(b_ref[i:i + 1, :], (8, HEAD_DIM))

    groups = []
    for r in range(HG_CHUNK // 8):
        if m >= 4:
            groups.append(row((8 * r // (2 * m)) * (2 * m) + m - 1))
        else:
            assert m == 2
            groups.append(jnp.where(sub < 4, row(8 * r + 1), row(8 * r + 5)))
    return jnp.concatenate(groups, axis=0)


def _hgrn_kernel(q_ref, f_ref, i_ref, g_ref, al_ref, cl_ref, on_ref, a_ref, m_ref,
                 o_ref, st_ref, b_ref, *, rows_per_step, heads_per_step):
    c = HG_CHUNK

    @pl.when(pl.program_id(2) == 0)
    def _():
        st_ref[...] = jnp.zeros_like(st_ref)

    def chunk(ci, carry):
        rows = pl.ds(pl.multiple_of(ci * c, c), c)
        heads = range(heads_per_step)
        cols = [slice(hh * HEAD_DIM, (hh + 1) * HEAD_DIM) for hh in heads]
        odd_row = (lax.broadcasted_iota(jnp.int32, (c, HEAD_DIM), 0) & 1) == 1
        loaded = [(f_ref[rows, cs], q_ref[rows, cs], i_ref[rows, cs], g_ref[rows, cs], st_ref[hh])
                  for hh, cs in zip(heads, cols)]
        fs, kks, bs = [], [], []
        for hh in heads:
            fp = loaded[hh][0]
            ls = jnp.minimum(fp, 0.0) - jnp.log1p(jnp.exp(-jnp.abs(fp)))
            x2 = cl_ref[:, cols[hh]] + ls
            al = al_ref[:, cols[hh]]
            logf = jnp.maximum(al, x2) + jnp.log1p(jnp.exp(-jnp.abs(al - x2)))
            fs.append(jnp.exp(logf))
            kks.append(1.0 - fs[hh])
            bs.append(_dot(a_ref[...], jnp.concatenate(_split3_bf16(logf), axis=0)))
        for hh in heads:
            b_ref[hh] = bs[hh]
        qs, ps = [], []
        for hh in heads:
            q = _silu(loaded[hh][1])
            kk = kks[hh]
            p = _dot_nt(q.astype(BF16), kk.astype(BF16)) * m_ref[0]
            for l, m in enumerate(HG_LEVELS):
                if m == 1:
                    el = jnp.where(odd_row, fs[hh], 1.0)
                else:
                    d = bs[hh] - _level_refs(b_ref.at[hh], m)
                    el = jnp.exp(pltpu.bitcast(pltpu.bitcast(d, jnp.int32) | jnp.int32(-2 ** 31), F32))
                p = p + _dot_nt((q * el).astype(BF16), (kk * el).astype(BF16)) * m_ref[l + 1]
            qs.append(q)
            ps.append(p)
        results = []
        for hh in heads:
            b, q, st = bs[hh], qs[hh], loaded[hh][4]
            v = loaded[hh][2].astype(BF16)
            b_last = jnp.broadcast_to(b_ref[hh, c - 1:c, :], (c, HEAD_DIM))
            o = _dot(ps[hh].astype(BF16), v) + _dot_nt((q * jnp.exp(b)).astype(BF16), st.astype(BF16))
            kdec = (kks[hh] * jnp.exp(b_last - b)).astype(BF16)
            st_new = st * jnp.exp(b_last[0:1]) + _dot_tn(v, kdec)
            results.append((o, st_new))
        for hh in heads:
            o = _rms(results[hh][0], on_ref[:, cols[hh]]) * _silu(loaded[hh][3])
            results[hh] = (o.astype(o_ref.dtype), results[hh][1])
        for hh in heads:
            o_ref[rows, cols[hh]] = results[hh][0]
            st_ref[hh] = results[hh][1]
        return carry

    lax.fori_loop(0, rows_per_step // c, chunk, 0)


def _hgrn_core(proj, log_lb, log1m_lb, onorm_g, batch, seq, rows_per_step=512, heads_per_step=16):
    m = proj.shape[0]
    tr = min(rows_per_step, seq)
    hb = heads_per_step
    w = hb * HEAD_DIM
    sec = D_MODEL // w
    rt = seq // tr
    a_np, m_np = _hgrn_tables()
    a_tab = jnp.asarray(a_np, BF16)
    m_tab = jnp.asarray(m_np, F32)

    def sect(s):
        return pl.BlockSpec((tr, w), lambda b, h, r, s=s: (b * rt + r, s * sec + h))

    vec = pl.BlockSpec((1, w), lambda b, h, r: (0, h))
    kern = functools.partial(_hgrn_kernel, rows_per_step=tr, heads_per_step=hb)
    return pl.pallas_call(
        kern,
        grid=(batch, N_HEADS // hb, rt),
        in_specs=[sect(0), sect(1), sect(2), sect(3), vec, vec, vec,
                  pl.BlockSpec(a_np.shape, lambda b, h, r: (0, 0)),
                  pl.BlockSpec(m_np.shape, lambda b, h, r: (0, 0, 0))],
        out_specs=pl.BlockSpec((tr, w), lambda b, h, r: (b * rt + r, h)),
        out_shape=jax.ShapeDtypeStruct((m, D_MODEL), BF16),
        scratch_shapes=[pltpu.VMEM((hb, HEAD_DIM, HEAD_DIM), F32), pltpu.VMEM((hb, HG_CHUNK, HEAD_DIM), F32)],
        compiler_params=_cparams(3),
        name="hgrn_core",
    )(proj, proj, proj, proj, log_lb, log1m_lb, onorm_g, a_tab, m_tab)


def _sba_kernel(q_ref, k_ref, v_ref, uo_ref, o_ref, c_ref, acc_ref, *, block_q, block_k, heads_per_step):
    qi = pl.program_id(2)
    heads = range(heads_per_step)
    cols = [slice(hh * HEAD_DIM, (hh + 1) * HEAD_DIM) for hh in heads]
    nkb = block_q // block_k

    def chunk(s0, diagonal, state):
        if diagonal:
            row = lax.broadcasted_iota(jnp.int32, (block_q, block_k), 0)
            lane = lax.broadcasted_iota(jnp.int32, (block_q, block_k), 1)
        zs = [_dot_nt(q_ref[:, cols[hh]], k_ref[pl.ds(s0, block_q), cols[hh]]) for hh in heads]
        carries, probs = [], []
        for hh in heads:
            c = state[hh][0]
            a_blocks = [None] * nkb
            for kb in reversed(range(nkb)):
                zb = zs[hh][:, kb * block_k:(kb + 1) * block_k]
                nz = pltpu.bitcast(pltpu.bitcast(zb, jnp.int32) | jnp.int32(-2 ** 31), F32)
                lg = jnp.log2(1.0 + jnp.exp2(nz))
                sp = jnp.maximum(zb, 0.0) + lg
                ls = jnp.minimum(zb, 0.0) - lg
                if diagonal:
                    strict = (kb * block_k + lane) < row
                    sp = jnp.where(strict, sp, 0.0)
                cs = _dot(sp.astype(BF16), uo_ref[...])
                a = jnp.exp2(ls - cs[:, :block_k] - c)
                if diagonal:
                    a = jnp.where(strict, a, 0.0)
                a_blocks[kb] = a.astype(BF16)
                c = c + cs[:, block_k:]
            carries.append(c)
            probs.append(jnp.concatenate(a_blocks, axis=1))
        return [(carries[hh], state[hh][1] + _dot(probs[hh], v_ref[pl.ds(s0, block_q), cols[hh]]))
                for hh in heads]

    zeros = jnp.zeros((block_q, block_k), F32)
    first = chunk(pl.multiple_of(qi * block_q, block_q), True, [(zeros, zeros)] * heads_per_step)
    for hh in heads:
        c_ref[hh] = first[hh][0]
        acc_ref[hh] = first[hh][1]

    def body(it, carry):
        s0 = pl.multiple_of((qi - 1 - it) * block_q, block_q)
        new = chunk(s0, False, [(c_ref[hh], acc_ref[hh]) for hh in heads])
        for hh in heads:
            c_ref[hh] = new[hh][0]
            acc_ref[hh] = new[hh][1]
        return carry

    lax.fori_loop(0, qi, body, 0)
    for hh in heads:
        o_ref[:, cols[hh]] = acc_ref[hh].astype(o_ref.dtype)


def _sba_core(qkv, batch, seq, block_q=512, block_k=128, heads_per_step=4):
    assert block_k == HEAD_DIM
    m = qkv.shape[0]
    bq = min(block_q, seq)
    hb = heads_per_step
    w = hb * HEAD_DIM
    sec = D_MODEL // w
    qt = seq // bq
    jj = np.arange(block_k)
    uo = np.concatenate([(jj[:, None] > jj[None, :]), np.ones((block_k, block_k), bool)], axis=1)
    uo = jnp.asarray(uo.astype(np.float32), BF16)
    kern = functools.partial(_sba_kernel, block_q=bq, block_k=block_k, heads_per_step=hb)
    return pl.pallas_call(
        kern,
        grid=(batch, N_HEADS // hb, qt),
        in_specs=[
            pl.BlockSpec((bq, w), lambda b, h, i: (b * qt + i, h)),
            pl.BlockSpec((seq, w), lambda b, h, i: (b, sec + h)),
            pl.BlockSpec((seq, w), lambda b, h, i: (b, 2 * sec + h)),
            pl.BlockSpec(uo.shape, lambda b, h, i: (0, 0)),
        ],
        out_specs=pl.BlockSpec((bq, w), lambda b, h, i: (b * qt + i, h)),
        out_shape=jax.ShapeDtypeStruct((m, D_MODEL), BF16),
        scratch_shapes=[pltpu.VMEM((hb, bq, block_k), F32), pltpu.VMEM((hb, bq, HEAD_DIM), F32)],
        compiler_params=_cparams(3),
        name="sba_core",
    )(qkv, qkv, qkv, uo)


def _pool_kernel(h_ref, g0_ref, pw_ref, ps_ref, g1_ref, o_ref, ubuf_ref, y_ref,
                 *, tm, tiles_per_seq):
    ti = pl.program_id(0) % tiles_per_seq
    hist = POOL_HIST

    @pl.when(ti == 0)
    def _():
        ubuf_ref[0:hist, :] = jnp.zeros((hist, D_MODEL), F32)

    h = h_ref[...]
    ubuf_ref[hist:hist + tm, :] = _rms(h, g0_ref[...])
    pos = ti * tm + lax.broadcasted_iota(jnp.int32, (tm, POOL_GROUP), 0) + 1
    for gi, win in enumerate(POOL_WINDOWS):
        cols = slice(gi * POOL_GROUP, (gi + 1) * POOL_GROUP)
        u = ubuf_ref[hist:hist + tm, cols]
        ws = u
        for dlt in range(1, win):
            ws = ws + ubuf_ref[hist - dlt:hist - dlt + tm, cols]
        p = ws / jnp.minimum(pos, win).astype(F32) - u
        y_ref[:, cols] = _dot(p.astype(BF16), pw_ref[gi]) * ps_ref[:, cols]
    o_ref[...] = h + _rms(y_ref[...], g1_ref[...])
    ubuf_ref[0:hist, :] = ubuf_ref[tm:tm + hist, :]


def _pool_layer(h, g0, pool_w, layer, pool_scale, g1, seq, tm=256):
    m, d = h.shape
    tm = min(tm, seq)
    kern = functools.partial(_pool_kernel, tm=tm, tiles_per_seq=seq // tm)
    row = pl.BlockSpec((1, d), lambda i: (0, 0))
    return pl.pallas_call(
        kern,
        grid=(m // tm,),
        in_specs=[
            pl.BlockSpec((tm, d), lambda i: (i, 0)),
            row,
            pl.BlockSpec((None,) + pool_w.shape[1:], lambda i: (layer, 0, 0, 0)),
            row,
            row,
        ],
        out_specs=pl.BlockSpec((tm, d), lambda i: (i, 0)),
        out_shape=jax.ShapeDtypeStruct((m, d), F32),
        scratch_shapes=[pltpu.VMEM((tm + POOL_HIST, d), F32), pltpu.VMEM((tm, d), F32)],
        compiler_params=_cparams(1),
        name="pool_layer",
    )(h, g0, pool_w, pool_scale, g1)


def _ffn_kernel(h_ref, g2_ref, wg_ref, wv_ref, cwg_ref, cwv_ref, cbg_ref, cbv_ref, wd_ref, g3_ref,
                o_ref, xn_ref, acc_ref, ug_ref, uv_ref, carry_ref, *, tm, tf, tiles_per_seq):
    i = pl.program_id(0)
    j = pl.program_id(1)
    hist = CONV_HIST
    first = (i % tiles_per_seq) == 0
    rp = tm // FFN_ROW_PARTS

    @pl.when(j == 0)
    def _():
        xn_ref[...] = _rms(h_ref[...], g2_ref[...]).astype(BF16)
        acc_ref[...] = jnp.zeros_like(acc_ref)

    ug_ref[0:hist, :] = jnp.where(first, 0.0, carry_ref[j, :, 0:tf])
    uv_ref[0:hist, :] = jnp.where(first, 0.0, carry_ref[j, :, tf:2 * tf])
    for p in range(FFN_ROW_PARTS):
        ug_ref[hist + p * rp:hist + (p + 1) * rp, :] = _dot(xn_ref[p * rp:(p + 1) * rp, :], wg_ref[...])
        uv_ref[hist + p * rp:hist + (p + 1) * rp, :] = _dot(xn_ref[p * rp:(p + 1) * rp, :], wv_ref[...])
    carry_ref[j, :, 0:tf] = ug_ref[tm:tm + hist, :]
    carry_ref[j, :, tf:2 * tf] = uv_ref[tm:tm + hist, :]

    def conv(u_ref, cw_ref, cb_ref, r0):
        return (cw_ref[0:1, :] * u_ref[r0 - 2:r0 - 2 + rp, :] + cw_ref[1:2, :] * u_ref[r0 - 1:r0 - 1 + rp, :]
                + cw_ref[2:3, :] * u_ref[r0:r0 + rp, :] + cb_ref[...])

    for p in range(FFN_ROW_PARTS):
        r0 = hist + p * rp
        act = (_silu(conv(ug_ref, cwg_ref, cbg_ref, r0)) * conv(uv_ref, cwv_ref, cbv_ref, r0)).astype(BF16)
        acc_ref[p * rp:(p + 1) * rp, :] += _dot(act, wd_ref[...])

    @pl.when(j == pl.num_programs(1) - 1)
    def _():
        o_ref[...] = h_ref[...] + _rms(acc_ref[...], g3_ref[...])


def _ffn_layer(h, g2, w_up, conv_w, conv_b, w_down, layer, g3, seq, tm=512, tf=512):
    m, d = h.shape
    tm = min(tm, seq)
    nf = D_FF // tf
    kern = functools.partial(_ffn_kernel, tm=tm, tf=tf, tiles_per_seq=seq // tm)
    row = pl.BlockSpec((1, d), lambda i, j: (0, 0))

    def up_side(rows, half):
        return pl.BlockSpec((None, rows, tf), lambda i, j: (layer, 0, half * nf + j))

    return pl.pallas_call(
        kern,
        grid=(m // tm, nf),
        in_specs=[
            pl.BlockSpec((tm, d), lambda i, j: (i, 0)),
            row,
            up_side(d, 0), up_side(d, 1),
            up_side(3, 0), up_side(3, 1),
            up_side(1, 0), up_side(1, 1),
            pl.BlockSpec((None, tf, d), lambda i, j: (layer, j, 0)),
            row,
        ],
        out_specs=pl.BlockSpec((tm, d), lambda i, j: (i, 0)),
        out_shape=jax.ShapeDtypeStruct((m, d), F32),
        scratch_shapes=[
            pltpu.VMEM((tm, d), BF16),
            pltpu.VMEM((tm, d), F32),
            pltpu.VMEM((tm + CONV_HIST, tf), F32),
            pltpu.VMEM((tm + CONV_HIST, tf), F32),
            pltpu.VMEM((nf, CONV_HIST, 2 * tf), F32),
        ],
        compiler_params=_cparams(2),
        name="ffn_layer",
    )(h, g2, w_up, w_up, conv_w, conv_w, conv_b, conv_b, w_down, g3)


def kernel(x, norm_g, hgrn_lb_logits, hgrn_w_in, hgrn_onorm_g, hgrn_w_out, sba_w_qkv, sba_w_out,
           pool_w, pool_scale, ffn_w_up, ffn_conv_w, ffn_conv_b, ffn_w_down):
    batch, seq, d = x.shape
    depth = norm_g.shape[0]
    lb_soft = jax.nn.softmax(hgrn_lb_logits.astype(F32), axis=0)
    lower = jnp.concatenate([jnp.zeros_like(lb_soft[:1]), jnp.cumsum(lb_soft[1:], axis=0)], axis=0)
    log_lb = jnp.log(lower)
    log1m_lb = jnp.log1p(-lower)

    hgrn_w_in, hgrn_w_out, sba_w_qkv, sba_w_out, pool_w, ffn_w_up, ffn_w_down = (
        w.astype(BF16) for w in (hgrn_w_in, hgrn_w_out, sba_w_qkv, sba_w_out, pool_w, ffn_w_up, ffn_w_down))
    ffn_conv_b = ffn_conv_b[:, None, :]

    h = x.reshape(batch * seq, d)
    for i in range(depth):
        kind, j = i % 3, i // 3
        g = norm_g[i].reshape(4, 1, d)
        if kind == 0:
            proj = _norm_matmul(h, g[0], hgrn_w_in, j, F32)
            mix = _hgrn_core(proj, log_lb[i:i + 1], log1m_lb[i:i + 1], hgrn_onorm_g[j][None, :], batch, seq)
            h = _proj_residual(mix, hgrn_w_out, j, g[1], h)
        elif kind == 1:
            qkv = _norm_matmul(h, g[0], sba_w_qkv, j, BF16,
                               scaled_cols=D_MODEL, scale=float(HEAD_DIM) ** -0.5 * LOG2_E)
            mix = _sba_core(qkv, batch, seq)
            h = _proj_residual(mix, sba_w_out, j, g[1], h)
        else:
            h = _pool_layer(h, g[0], pool_w, j, pool_scale[j][None, :], g[1], seq)
        h = _ffn_layer(h, g[2], ffn_w_up, ffn_conv_w, ffn_conv_b, ffn_w_down, i, g[3], seq)
    return h.reshape(batch, seq, d)
```

```python
import functools

import numpy as np
import jax
import jax.numpy as jnp
from jax import lax
from jax.experimental import pallas as pl
from jax.experimental.pallas import tpu as pltpu

F32 = jnp.float32
BF16 = jnp.bfloat16

D_MODEL = 2048
NORM_EPS = 1e-6
LOG2_E = 1.4426950408889634
HEAD_DIM = 128
N_HEADS = D_MODEL // HEAD_DIM
HG_CHUNK = 64
HG_LEVELS = (32, 16, 8, 4, 2, 1)
POOL_WINDOWS = (2, 4, 8, 16)
POOL_GROUP = D_MODEL // len(POOL_WINDOWS)
POOL_HIST = 16
D_FF = 5632
FFN_ROW_PARTS = 2
CONV_HIST = 8
VMEM_LIMIT_BYTES = 56 * 1024 * 1024


def _cparams(n_axes):
    return pltpu.CompilerParams(
        dimension_semantics=("arbitrary",) * n_axes,
        vmem_limit_bytes=VMEM_LIMIT_BYTES)


def _rms(x, g):
    ms = jnp.mean(x * x, axis=-1, keepdims=True)
    return x * lax.rsqrt(ms + NORM_EPS) * g


def _silu(x):
    return x / (1.0 + jnp.exp(-x))


def _dot(a, b):
    return jnp.dot(a, b, preferred_element_type=F32)


def _dot_nt(a, b):
    return lax.dot_general(a, b, (((1,), (1,)), ((), ())), preferred_element_type=F32)


def _dot_tn(a, b):
    return lax.dot_general(a, b, (((0,), (0,)), ((), ())), preferred_element_type=F32)


def _split3_bf16(x):
    hi = x.astype(BF16)
    r = x - hi.astype(F32)
    mid = r.astype(BF16)
    lo = (r - mid.astype(F32)).astype(BF16)
    return hi, mid, lo


def _norm_matmul_kernel(x_ref, g_ref, w_ref, o_ref, xn_ref, *, scaled_blocks, scale):
    @pl.when(pl.program_id(1) == 0)
    def _():
        xn_ref[...] = _rms(x_ref[...], g_ref[...]).astype(BF16)

    y = _dot(xn_ref[...], w_ref[...])
    if scaled_blocks:
        y = y * jnp.where(pl.program_id(1) < scaled_blocks, scale, 1.0)
    o_ref[...] = y.astype(o_ref.dtype)


def _norm_matmul(x, g, w, layer, out_dtype, scaled_cols=0, scale=1.0, tm=1024, tn=1024):
    m, d = x.shape
    n = w.shape[2]
    kern = functools.partial(_norm_matmul_kernel, scaled_blocks=scaled_cols // tn, scale=scale)
    return pl.pallas_call(
        kern,
        grid=(m // tm, n // tn),
        in_specs=[
            pl.BlockSpec((tm, d), lambda i, j: (i, 0)),
            pl.BlockSpec((1, d), lambda i, j: (0, 0)),
            pl.BlockSpec((None, d, tn), lambda i, j: (layer, 0, j)),
        ],
        out_specs=pl.BlockSpec((tm, tn), lambda i, j: (i, j)),
        out_shape=jax.ShapeDtypeStruct((m, n), out_dtype),
        scratch_shapes=[pltpu.VMEM((tm, d), BF16)],
        compiler_params=_cparams(2),
        name="norm_matmul",
    )(x, g, w)


def _proj_residual_kernel(a_ref, w_ref, g_ref, h_ref, o_ref):
    rp = a_ref.shape[0] // 2
    ys = [_dot(a_ref[p * rp:(p + 1) * rp, :], w_ref[...]) for p in range(2)]
    for p in range(2):
        rows = slice(p * rp, (p + 1) * rp)
        o_ref[rows, :] = h_ref[rows, :] + _rms(ys[p], g_ref[...])


def _proj_residual(a, w, layer, g, h, tm=512):
    m, k = a.shape
    d = w.shape[2]
    return pl.pallas_call(
        _proj_residual_kernel,
        grid=(m // tm,),
        in_specs=[
            pl.BlockSpec((tm, k), lambda i: (i, 0)),
            pl.BlockSpec((None, k, d), lambda i: (layer, 0, 0)),
            pl.BlockSpec((1, d), lambda i: (0, 0)),
            pl.BlockSpec((tm, d), lambda i: (i, 0)),
        ],
        out_specs=pl.BlockSpec((tm, d), lambda i: (i, 0)),
        out_shape=jax.ShapeDtypeStruct((m, d), F32),
        compiler_params=_cparams(1),
        name="proj_residual",
    )(a, w, g, h)


def _hgrn_tables():
    c = HG_CHUNK
    t = np.arange(c)[:, None]
    j = np.arange(c)[None, :]
    masks = [(t == j)]
    for m in HG_LEVELS:
        second = (t % (2 * m)) >= m
        masks.append(((t // (2 * m)) == (j // (2 * m))) & second & ((j % (2 * m)) < m))
    a = (j <= t).astype(np.float32)
    a = np.concatenate([a, a, a], axis=1)
    msk = np.stack(masks, axis=0).astype(np.float32)
    return a, msk


def _level_refs(b_ref, m):
    sub = lax.broadcasted_iota(jnp.int32, (8, HEAD_DIM), 0)

    def row(i):
        return jnp.broadcast_to(b_ref[i:i + 1, :], (8, HEAD_DIM))

    groups = []
    for r in range(HG_CHUNK // 8):
        if m >= 4:
            groups.append(row((8 * r // (2 * m)) * (2 * m) + m - 1))
        else:
            assert m == 2
            groups.append(jnp.where(sub < 4, row(8 * r + 1), row(8 * r + 5)))
    return jnp.concatenate(groups, axis=0)


def _hgrn_kernel(q_ref, f_ref, i_ref, g_ref, al_ref, cl_ref, on_ref, a_ref, m_ref,
                 o_ref, st_ref, b_ref, *, rows_per_step, heads_per_step):
    c = HG_CHUNK

    @pl.when(pl.program_id(2) == 0)
    def _():
        st_ref[...] = jnp.zeros_like(st_ref)

    def chunk(ci, carry):
        rows = pl.ds(pl.multiple_of(ci * c, c), c)
        heads = range(heads_per_step)
        cols = [slice(hh * HEAD_DIM, (hh + 1) * HEAD_DIM) for hh in heads]
        odd_row = (lax.broadcasted_iota(jnp.int32, (c, HEAD_DIM), 0) & 1) == 1
        loaded = [(f_ref[rows, cs], q_ref[rows, cs], i_ref[rows, cs], g_ref[rows, cs], st_ref[hh])
                  for hh, cs in zip(heads, cols)]
        fs, kks, bs = [], [], []
        for hh in heads:
            fp = loaded[hh][0]
            ls = jnp.minimum(fp, 0.0) - jnp.log1p(jnp.exp(-jnp.abs(fp)))
            x2 = cl_ref[:, cols[hh]] + ls
            al = al_ref[:, cols[hh]]
            logf = jnp.maximum(al, x2) + jnp.log1p(jnp.exp(-jnp.abs(al - x2)))
            fs.append(jnp.exp(logf))
            kks.append(1.0 - fs[hh])
            bs.append(_dot(a_ref[...], jnp.concatenate(_split3_bf16(logf), axis=0)))
        for hh in heads:
            b_ref[hh] = bs[hh]
        qs, ps = [], []
        for hh in heads:
            q = _silu(loaded[hh][1])
            kk = kks[hh]
            p = _dot_nt(q.astype(BF16), kk.astype(BF16)) * m_ref[0]
            for l, m in enumerate(HG_LEVELS):
                if m == 1:
                    el = jnp.where(odd_row, fs[hh], 1.0)
                else:
                    d = bs[hh] - _level_refs(b_ref.at[hh], m)
                    el = jnp.exp(pltpu.bitcast(pltpu.bitcast(d, jnp.int32) | jnp.int32(-2 ** 31), F32))
                p = p + _dot_nt((q * el).astype(BF16), (kk * el).astype(BF16)) * m_ref[l + 1]
            qs.append(q)
            ps.append(p)
        results = []
        for hh in heads:
            b, q, st = bs[hh], qs[hh], loaded[hh][4]
            v = loaded[hh][2].astype(BF16)
            b_last = jnp.broadcast_to(b_ref[hh, c - 1:c, :], (c, HEAD_DIM))
            o = _dot(ps[hh].astype(BF16), v) + _dot_nt((q * jnp.exp(b)).astype(BF16), st.astype(BF16))
            kdec = (kks[hh] * jnp.exp(b_last - b)).astype(BF16)
            st_new = st * jnp.exp(b_last[0:1]) + _dot_tn(v, kdec)
            results.append((o, st_new))
        for hh in heads:
            o = _rms(results[hh][0], on_ref[:, cols[hh]]) * _silu(loaded[hh][3])
            results[hh] = (o.astype(o_ref.dtype), results[hh][1])
        for hh in heads:
            o_ref[rows, cols[hh]] = results[hh][0]
            st_ref[hh] = results[hh][1]
        return carry

    lax.fori_loop(0, rows_per_step // c, chunk, 0)


def _hgrn_core(proj, log_lb, log1m_lb, onorm_g, batch, seq, rows_per_step=512, heads_per_step=16):
    m = proj.shape[0]
    tr = min(rows_per_step, seq)
    hb = heads_per_step
    w = hb * HEAD_DIM
    sec = D_MODEL // w
    rt = seq // tr
    a_np, m_np = _hgrn_tables()
    a_tab = jnp.asarray(a_np, BF16)
    m_tab = jnp.asarray(m_np, F32)

    def sect(s):
        return pl.BlockSpec((tr, w), lambda b, h, r, s=s: (b * rt + r, s * sec + h))

    vec = pl.BlockSpec((1, w), lambda b, h, r: (0, h))
    kern = functools.partial(_hgrn_kernel, rows_per_step=tr, heads_per_step=hb)
    return pl.pallas_call(
        kern,
        grid=(batch, N_HEADS // hb, rt),
        in_specs=[sect(0), sect(1), sect(2), sect(3), vec, vec, vec,
                  pl.BlockSpec(a_np.shape, lambda b, h, r: (0, 0)),
                  pl.BlockSpec(m_np.shape, lambda b, h, r: (0, 0, 0))],
        out_specs=pl.BlockSpec((tr, w), lambda b, h, r: (b * rt + r, h)),
        out_shape=jax.ShapeDtypeStruct((m, D_MODEL), BF16),
        scratch_shapes=[pltpu.VMEM((hb, HEAD_DIM, HEAD_DIM), F32), pltpu.VMEM((hb, HG_CHUNK, HEAD_DIM), F32)],
        compiler_params=_cparams(3),
        name="hgrn_core",
    )(proj, proj, proj, proj, log_lb, log1m_lb, onorm_g, a_tab, m_tab)


def _sba_kernel(q_ref, k_ref, v_ref, uo_ref, o_ref, c_ref, acc_ref, *, block_q, block_k, heads_per_step):
    qi = pl.program_id(2)
    heads = range(heads_per_step)
    cols = [slice(hh * HEAD_DIM, (hh + 1) * HEAD_DIM) for hh in heads]
    nkb = block_q // block_k

    def chunk(s0, diagonal, state):
        if diagonal:
            row = lax.broadcasted_iota(jnp.int32, (block_q, block_k), 0)
            lane = lax.broadcasted_iota(jnp.int32, (block_q, block_k), 1)
        zs = [_dot_nt(q_ref[:, cols[hh]], k_ref[pl.ds(s0, block_q), cols[hh]]) for hh in heads]
        carries, probs = [], []
        for hh in heads:
            c = state[hh][0]
            a_blocks = [None] * nkb
            for kb in reversed(range(nkb)):
                zb = zs[hh][:, kb * block_k:(kb + 1) * block_k]
                nz = pltpu.bitcast(pltpu.bitcast(zb, jnp.int32) | jnp.int32(-2 ** 31), F32)
                lg = jnp.log2(1.0 + jnp.exp2(nz))
                sp = jnp.maximum(zb, 0.0) + lg
                ls = jnp.minimum(zb, 0.0) - lg
                if diagonal:
                    strict = (kb * block_k + lane) < row
                    sp = jnp.where(strict, sp, 0.0)
                cs = _dot(sp.astype(BF16), uo_ref[...])
                a = jnp.exp2(ls - cs[:, :block_k] - c)
                if diagonal:
                    a = jnp.where(strict, a, 0.0)
                a_blocks[kb] = a.astype(BF16)
                c = c + cs[:, block_k:]
            carries.append(c)
            probs.append(jnp.concatenate(a_blocks, axis=1))
        return [(carries[hh], state[hh][1] + _dot(probs[hh], v_ref[pl.ds(s0, block_q), cols[hh]]))
                for hh in heads]

    zeros = jnp.zeros((block_q, block_k), F32)
    first = chunk(pl.multiple_of(qi * block_q, block_q), True, [(zeros, zeros)] * heads_per_step)
    for hh in heads:
        c_ref[hh] = first[hh][0]
        acc_ref[hh] = first[hh][1]

    def body(it, carry):
        s0 = pl.multiple_of((qi - 1 - it) * block_q, block_q)
        new = chunk(s0, False, [(c_ref[hh], acc_ref[hh]) for hh in heads])
        for hh in heads:
            c_ref[hh] = new[hh][0]
            acc_ref[hh] = new[hh][1]
        return carry

    lax.fori_loop(0, qi, body, 0)
    for hh in heads:
        o_ref[:, cols[hh]] = acc_ref[hh].astype(o_ref.dtype)


def _sba_core(qkv, batch, seq, block_q=512, block_k=128, heads_per_step=4):
    assert block_k == HEAD_DIM
    m = qkv.shape[0]
    bq = min(block_q, seq)
    hb = heads_per_step
    w = hb * HEAD_DIM
    sec = D_MODEL // w
    qt = seq // bq
    jj = np.arange(block_k)
    uo = np.concatenate([(jj[:, None] > jj[None, :]), np.ones((block_k, block_k), bool)], axis=1)
    uo = jnp.asarray(uo.astype(np.float32), BF16)
    kern = functools.partial(_sba_kernel, block_q=bq, block_k=block_k, heads_per_step=hb)
    return pl.pallas_call(
        kern,
        grid=(batch, N_HEADS // hb, qt),
        in_specs=[
            pl.BlockSpec((bq, w), lambda b, h, i: (b * qt + i, h)),
            pl.BlockSpec((seq, w), lambda b, h, i: (b, sec + h)),
            pl.BlockSpec((seq, w), lambda b, h, i: (b, 2 * sec + h)),
            pl.BlockSpec(uo.shape, lambda b, h, i: (0, 0)),
        ],
        out_specs=pl.BlockSpec((bq, w), lambda b, h, i: (b * qt + i, h)),
        out_shape=jax.ShapeDtypeStruct((m, D_MODEL), BF16),
        scratch_shapes=[pltpu.VMEM((hb, bq, block_k), F32), pltpu.VMEM((hb, bq, HEAD_DIM), F32)],
        compiler_params=_cparams(3),
        name="sba_core",
    )(qkv, qkv, qkv, uo)


def _pool_kernel(h_ref, g0_ref, pw_ref, ps_ref, g1_ref, o_ref, ubuf_ref, y_ref,
                 *, tm, tiles_per_seq):
    ti = pl.program_id(0) % tiles_per_seq
    hist = POOL_HIST

    @pl.when(ti == 0)
    def _():
        ubuf_ref[0:hist, :] = jnp.zeros((hist, D_MODEL), F32)

    h = h_ref[...]
    ubuf_ref[hist:hist + tm, :] = _rms(h, g0_ref[...])
    pos = ti * tm + lax.broadcasted_iota(jnp.int32, (tm, POOL_GROUP), 0) + 1
    for gi, win in enumerate(POOL_WINDOWS):
        cols = slice(gi * POOL_GROUP, (gi + 1) * POOL_GROUP)
        u = ubuf_ref[hist:hist + tm, cols]
        ws = u
        for dlt in range(1, win):
            ws = ws + ubuf_ref[hist - dlt:hist - dlt + tm, cols]
        p = ws / jnp.minimum(pos, win).astype(F32) - u
        y_ref[:, cols] = _dot(p.astype(BF16), pw_ref[gi]) * ps_ref[:, cols]
    o_ref[...] = h + _rms(y_ref[...], g1_ref[...])
    ubuf_ref[0:hist, :] = ubuf_ref[tm:tm + hist, :]


def _pool_layer(h, g0, pool_w, layer, pool_scale, g1, seq, tm=256):
    m, d = h.shape
    tm = min(tm, seq)
    kern = functools.partial(_pool_kernel, tm=tm, tiles_per_seq=seq // tm)
    row = pl.BlockSpec((1, d), lambda i: (0, 0))
    return pl.pallas_call(
        kern,
        grid=(m // tm,),
        in_specs=[
            pl.BlockSpec((tm, d), lambda i: (i, 0)),
            row,
            pl.BlockSpec((None,) + pool_w.shape[1:], lambda i: (layer, 0, 0, 0)),
            row,
            row,
        ],
        out_specs=pl.BlockSpec((tm, d), lambda i: (i, 0)),
        out_shape=jax.ShapeDtypeStruct((m, d), F32),
        scratch_shapes=[pltpu.VMEM((tm + POOL_HIST, d), F32), pltpu.VMEM((tm, d), F32)],
        compiler_params=_cparams(1),
        name="pool_layer",
    )(h, g0, pool_w, pool_scale, g1)


def _ffn_kernel(h_ref, g2_ref, wg_ref, wv_ref, cwg_ref, cwv_ref, cbg_ref, cbv_ref, wd_ref, g3_ref,
                o_ref, xn_ref, acc_ref, ug_ref, uv_ref, carry_ref, *, tm, tf, tiles_per_seq):
    i = pl.program_id(0)
    j = pl.program_id(1)
    hist = CONV_HIST
    first = (i % tiles_per_seq) == 0
    rp = tm // FFN_ROW_PARTS
    parts = [slice(p * rp, (p + 1) * rp) for p in range(FFN_ROW_PARTS)]
    last = pl.num_programs(1) - 1

    def conv(u_ref, cw_ref, cb_ref, r0):
        return (cw_ref[0:1, :] * u_ref[r0 - 2:r0 - 2 + rp, :] + cw_ref[1:2, :] * u_ref[r0 - 1:r0 - 1 + rp, :]
                + cw_ref[2:3, :] * u_ref[r0:r0 + rp, :] + cb_ref[...])

    def step(is_first, is_last):
        if is_first:
            for rows in parts:
                xn_ref[rows, :] = _rms(h_ref[rows, :], g2_ref[...]).astype(BF16)
        ug_ref[0:hist, :] = jnp.where(first, 0.0, carry_ref[j, :, 0:tf])
        uv_ref[0:hist, :] = jnp.where(first, 0.0, carry_ref[j, :, tf:2 * tf])
        for p, rows in enumerate(parts):
            ug_ref[hist + p * rp:hist + (p + 1) * rp, :] = _dot(xn_ref[rows, :], wg_ref[...])
            uv_ref[hist + p * rp:hist + (p + 1) * rp, :] = _dot(xn_ref[rows, :], wv_ref[...])
        carry_ref[j, :, 0:tf] = ug_ref[tm:tm + hist, :]
        carry_ref[j, :, tf:2 * tf] = uv_ref[tm:tm + hist, :]
        for p, rows in enumerate(parts):
            r0 = hist + p * rp
            act = (_silu(conv(ug_ref, cwg_ref, cbg_ref, r0)) * conv(uv_ref, cwv_ref, cbv_ref, r0)).astype(BF16)
            y = _dot(act, wd_ref[...])
            if is_first:
                acc_ref[rows, :] = y
            elif is_last:
                o_ref[rows, :] = h_ref[rows, :] + _rms(acc_ref[rows, :] + y, g3_ref[...])
            else:
                acc_ref[rows, :] += y

    pl.when(j == 0)(functools.partial(step, True, False))
    pl.when(jnp.logical_and(j > 0, j < last))(functools.partial(step, False, False))
    pl.when(j == last)(functools.partial(step, False, True))


def _ffn_layer(h, g2, w_up, conv_w, conv_b, w_down, layer, g3, seq, tm=512, tf=512):
    m, d = h.shape
    tm = min(tm, seq)
    nf = D_FF // tf
    kern = functools.partial(_ffn_kernel, tm=tm, tf=tf, tiles_per_seq=seq // tm)
    row = pl.BlockSpec((1, d), lambda i, j: (0, 0))

    def up_side(rows, half):
        return pl.BlockSpec((None, rows, tf), lambda i, j: (layer, 0, half * nf + j))

    return pl.pallas_call(
        kern,
        grid=(m // tm, nf),
        in_specs=[
            pl.BlockSpec((tm, d), lambda i, j: (i, 0)),
            row,
            up_side(d, 0), up_side(d, 1),
            up_side(3, 0), up_side(3, 1),
            up_side(1, 0), up_side(1, 1),
            pl.BlockSpec((None, tf, d), lambda i, j: (layer, j, 0)),
            row,
        ],
        out_specs=pl.BlockSpec((tm, d), lambda i, j: (i, 0)),
        out_shape=jax.ShapeDtypeStruct((m, d), F32),
        scratch_shapes=[
            pltpu.VMEM((tm, d), BF16),
            pltpu.VMEM((tm, d), F32),
            pltpu.VMEM((tm + CONV_HIST, tf), F32),
            pltpu.VMEM((tm + CONV_HIST, tf), F32),
            pltpu.VMEM((nf, CONV_HIST, 2 * tf), F32),
        ],
        compiler_params=_cparams(2),
        name="ffn_layer",
    )(h, g2, w_up, w_up, conv_w, conv_w, conv_b, conv_b, w_down, g3)


def kernel(x, norm_g, hgrn_lb_logits, hgrn_w_in, hgrn_onorm_g, hgrn_w_out, sba_w_qkv, sba_w_out,
           pool_w, pool_scale, ffn_w_up, ffn_conv_w, ffn_conv_b, ffn_w_down):
    batch, seq, d = x.shape
    depth = norm_g.shape[0]
    lb_soft = jax.nn.softmax(hgrn_lb_logits.astype(F32), axis=0)
    lower = jnp.concatenate([jnp.zeros_like(lb_soft[:1]), jnp.cumsum(lb_soft[1:], axis=0)], axis=0)
    log_lb = jnp.log(lower)
    log1m_lb = jnp.log1p(-lower)

    hgrn_w_in, hgrn_w_out, sba_w_qkv, sba_w_out, pool_w, ffn_w_up, ffn_w_down = (
        w.astype(BF16) for w in (hgrn_w_in, hgrn_w_out, sba_w_qkv, sba_w_out, pool_w, ffn_w_up, ffn_w_down))
    ffn_conv_b = ffn_conv_b[:, None, :]

    h = x.reshape(batch * seq, d)
    for i in range(depth):
        kind, j = i % 3, i // 3
        g = norm_g[i].reshape(4, 1, d)
        if kind == 0:
            proj = _norm_matmul(h, g[0], hgrn_w_in, j, F32)
            mix = _hgrn_core(proj, log_lb[i:i + 1], log1m_lb[i:i + 1], hgrn_onorm_g[j][None, :], batch, seq)
            h = _proj_residual(mix, hgrn_w_out, j, g[1], h)
        elif kind == 1:
            qkv = _norm_matmul(h, g[0], sba_w_qkv, j, BF16,
                               scaled_cols=D_MODEL, scale=float(HEAD_DIM) ** -0.5 * LOG2_E)
            mix = _sba_core(qkv, batch, seq)
            h = _proj_residual(mix, sba_w_out, j, g[1], h)
        else:
            h = _pool_layer(h, g[0], pool_w, j, pool_scale[j][None, :], g[1], seq)
        h = _ffn_layer(h, g[2], ffn_w_up, ffn_conv_w, ffn_conv_b, ffn_w_down, i, g[3], seq)
    return h.reshape(batch, seq, d)
```

```python
import functools

import numpy as np
import jax
import jax.numpy as jnp
from jax import lax
from jax.experimental import pallas as pl
from jax.experimental.pallas import tpu as pltpu

F32 = jnp.float32
BF16 = jnp.bfloat16

D_MODEL = 2048
NORM_EPS = 1e-6
LOG2_E = 1.4426950408889634
HEAD_DIM = 128
N_HEADS = D_MODEL // HEAD_DIM
HG_CHUNK = 64
HG_LEVELS = (32, 16, 8, 4, 2, 1)
POOL_WINDOWS = (2, 4, 8, 16)
POOL_GROUP = D_MODEL // len(POOL_WINDOWS)
POOL_HIST = 16
D_FF = 5632
FFN_ROW_PARTS = 2
CONV_HIST = 8
VMEM_LIMIT_BYTES = 56 * 1024 * 1024


def _cparams(n_axes):
    return pltpu.CompilerParams(
        dimension_semantics=("arbitrary",) * n_axes,
        vmem_limit_bytes=VMEM_LIMIT_BYTES)


def _rms(x, g):
    ms = jnp.mean(x * x, axis=-1, keepdims=True)
    return x * lax.rsqrt(ms + NORM_EPS) * g


def _silu(x):
    return x / (1.0 + jnp.exp(-x))


def _dot(a, b):
    return jnp.dot(a, b, preferred_element_type=F32)


def _dot_nt(a, b):
    return lax.dot_general(a, b, (((1,), (1,)), ((), ())), preferred_element_type=F32)


def _dot_tn(a, b):
    return lax.dot_general(a, b, (((0,), (0,)), ((), ())), preferred_element_type=F32)


def _split3_bf16(x):
    hi = x.astype(BF16)
    r = x - hi.astype(F32)
    mid = r.astype(BF16)
    lo = (r - mid.astype(F32)).astype(BF16)
    return hi, mid, lo


def _norm_matmul_kernel(x_ref, g_ref, w_ref, o_ref, xn_ref, *, scaled_blocks, scale):
    rp = x_ref.shape[0] // 2
    parts = [slice(0, rp), slice(rp, 2 * rp)]

    def step(is_first):
        if is_first:
            for rows in parts:
                xn_ref[rows, :] = _rms(x_ref[rows, :], g_ref[...]).astype(BF16)
        for rows in parts:
            y = _dot(xn_ref[rows, :], w_ref[...])
            if scaled_blocks:
                y = y * jnp.where(pl.program_id(1) < scaled_blocks, scale, 1.0)
            o_ref[rows, :] = y.astype(o_ref.dtype)

    pl.when(pl.program_id(1) == 0)(functools.partial(step, True))
    pl.when(pl.program_id(1) != 0)(functools.partial(step, False))


def _norm_matmul(x, g, w, layer, out_dtype, scaled_cols=0, scale=1.0, tm=1024, tn=1024):
    m, d = x.shape
    n = w.shape[2]
    kern = functools.partial(_norm_matmul_kernel, scaled_blocks=scaled_cols // tn, scale=scale)
    return pl.pallas_call(
        kern,
        grid=(m // tm, n // tn),
        in_specs=[
            pl.BlockSpec((tm, d), lambda i, j: (i, 0)),
            pl.BlockSpec((1, d), lambda i, j: (0, 0)),
            pl.BlockSpec((None, d, tn), lambda i, j: (layer, 0, j)),
        ],
        out_specs=pl.BlockSpec((tm, tn), lambda i, j: (i, j)),
        out_shape=jax.ShapeDtypeStruct((m, n), out_dtype),
        scratch_shapes=[pltpu.VMEM((tm, d), BF16)],
        compiler_params=_cparams(2),
        name="norm_matmul",
    )(x, g, w)


def _proj_residual_kernel(a_ref, w_ref, g_ref, h_ref, o_ref):
    rp = a_ref.shape[0] // 2
    ys = [_dot(a_ref[p * rp:(p + 1) * rp, :], w_ref[...]) for p in range(2)]
    for p in range(2):
        rows = slice(p * rp, (p + 1) * rp)
        o_ref[rows, :] = h_ref[rows, :] + _rms(ys[p], g_ref[...])


def _proj_residual(a, w, layer, g, h, tm=512):
    m, k = a.shape
    d = w.shape[2]
    return pl.pallas_call(
        _proj_residual_kernel,
        grid=(m // tm,),
        in_specs=[
            pl.BlockSpec((tm, k), lambda i: (i, 0)),
            pl.BlockSpec((None, k, d), lambda i: (layer, 0, 0)),
            pl.BlockSpec((1, d), lambda i: (0, 0)),
            pl.BlockSpec((tm, d), lambda i: (i, 0)),
        ],
        out_specs=pl.BlockSpec((tm, d), lambda i: (i, 0)),
        out_shape=jax.ShapeDtypeStruct((m, d), F32),
        compiler_params=_cparams(1),
        name="proj_residual",
    )(a, w, g, h)


def _hgrn_tables():
    c = HG_CHUNK
    t = np.arange(c)[:, None]
    j = np.arange(c)[None, :]
    masks = [(t == j)]
    for m in HG_LEVELS:
        second = (t % (2 * m)) >= m
        masks.append(((t // (2 * m)) == (j // (2 * m))) & second & ((j % (2 * m)) < m))
    a = (j <= t).astype(np.float32)
    a = np.concatenate([a, a, a], axis=1)
    msk = np.stack(masks, axis=0).astype(np.float32)
    return a, msk


def _level_refs(b_ref, m):
    sub = lax.broadcasted_iota(jnp.int32, (8, HEAD_DIM), 0)

    def row(i):
        return jnp.broadcast_to(b_ref[i:i + 1, :], (8, HEAD_DIM))

    groups = []
    for r in range(HG_CHUNK // 8):
        if m >= 4:
            groups.append(row((8 * r // (2 * m)) * (2 * m) + m - 1))
        else:
            assert m == 2
            groups.append(jnp.where(sub < 4, row(8 * r + 1), row(8 * r + 5)))
    return jnp.concatenate(groups, axis=0)


def _hgrn_kernel(q_ref, f_ref, i_ref, g_ref, al_ref, cl_ref, on_ref, a_ref, m_ref,
                 o_ref, st_ref, b_ref, *, rows_per_step, heads_per_step):
    c = HG_CHUNK

    @pl.when(pl.program_id(2) == 0)
    def _():
        st_ref[...] = jnp.zeros_like(st_ref)

    def chunk(ci, carry):
        rows = pl.ds(pl.multiple_of(ci * c, c), c)
        heads = range(heads_per_step)
        cols = [slice(hh * HEAD_DIM, (hh + 1) * HEAD_DIM) for hh in heads]
        odd_row = (lax.broadcasted_iota(jnp.int32, (c, HEAD_DIM), 0) & 1) == 1
        loaded = [(f_ref[rows, cs], q_ref[rows, cs], i_ref[rows, cs], g_ref[rows, cs], st_ref[hh])
                  for hh, cs in zip(heads, cols)]
        fs, kks, bs = [], [], []
        for hh in heads:
            fp = loaded[hh][0]
            ls = jnp.minimum(fp, 0.0) - jnp.log1p(jnp.exp(-jnp.abs(fp)))
            x2 = cl_ref[:, cols[hh]] + ls
            al = al_ref[:, cols[hh]]
            logf = jnp.maximum(al, x2) + jnp.log1p(jnp.exp(-jnp.abs(al - x2)))
            fs.append(jnp.exp(logf))
            kks.append(1.0 - fs[hh])
            bs.append(_dot(a_ref[...], jnp.concatenate(_split3_bf16(logf), axis=0)))
        for hh in heads:
            b_ref[hh] = bs[hh]
        qs, ps = [], []
        for hh in heads:
            q = _silu(loaded[hh][1])
            kk = kks[hh]
            p = _dot_nt(q.astype(BF16), kk.astype(BF16)) * m_ref[0]
            for l, m in enumerate(HG_LEVELS):
                if m == 1:
                    el = jnp.where(odd_row, fs[hh], 1.0)
                else:
                    d = bs[hh] - _level_refs(b_ref.at[hh], m)
                    el = jnp.exp(pltpu.bitcast(pltpu.bitcast(d, jnp.int32) | jnp.int32(-2 ** 31), F32))
                p = p + _dot_nt((q * el).astype(BF16), (kk * el).astype(BF16)) * m_ref[l + 1]
            qs.append(q)
            ps.append(p)
        results = []
        for hh in heads:
            b, q, st = bs[hh], qs[hh], loaded[hh][4]
            v = loaded[hh][2].astype(BF16)
            b_last = jnp.broadcast_to(b_ref[hh, c - 1:c, :], (c, HEAD_DIM))
            o = _dot(ps[hh].astype(BF16), v) + _dot_nt((q * jnp.exp(b)).astype(BF16), st.astype(BF16))
            kdec = (kks[hh] * jnp.exp(b_last - b)).astype(BF16)
            st_new = st * jnp.exp(b_last[0:1]) + _dot_tn(v, kdec)
            results.append((o, st_new))
        for hh in heads:
            o = _rms(results[hh][0], on_ref[:, cols[hh]]) * _silu(loaded[hh][3])
            results[hh] = (o.astype(o_ref.dtype), results[hh][1])
        for hh in heads:
            o_ref[rows, cols[hh]] = results[hh][0]
            st_ref[hh] = results[hh][1]
        return carry

    lax.fori_loop(0, rows_per_step // c, chunk, 0)


def _hgrn_core(proj, log_lb, log1m_lb, onorm_g, batch, seq, rows_per_step=512, heads_per_step=16):
    m = proj.shape[0]
    tr = min(rows_per_step, seq)
    hb = heads_per_step
    w = hb * HEAD_DIM
    sec = D_MODEL // w
    rt = seq // tr
    a_np, m_np = _hgrn_tables()
    a_tab = jnp.asarray(a_np, BF16)
    m_tab = jnp.asarray(m_np, F32)

    def sect(s):
        return pl.BlockSpec((tr, w), lambda b, h, r, s=s: (b * rt + r, s * sec + h))

    vec = pl.BlockSpec((1, w), lambda b, h, r: (0, h))
    kern = functools.partial(_hgrn_kernel, rows_per_step=tr, heads_per_step=hb)
    return pl.pallas_call(
        kern,
        grid=(batch, N_HEADS // hb, rt),
        in_specs=[sect(0), sect(1), sect(2), sect(3), vec, vec, vec,
                  pl.BlockSpec(a_np.shape, lambda b, h, r: (0, 0)),
                  pl.BlockSpec(m_np.shape, lambda b, h, r: (0, 0, 0))],
        out_specs=pl.BlockSpec((tr, w), lambda b, h, r: (b * rt + r, h)),
        out_shape=jax.ShapeDtypeStruct((m, D_MODEL), BF16),
        scratch_shapes=[pltpu.VMEM((hb, HEAD_DIM, HEAD_DIM), F32), pltpu.VMEM((hb, HG_CHUNK, HEAD_DIM), F32)],
        compiler_params=_cparams(3),
        name="hgrn_core",
    )(proj, proj, proj, proj, log_lb, log1m_lb, onorm_g, a_tab, m_tab)


def _sba_kernel(q_ref, k_ref, v_ref, uo_ref, o_ref, c_ref, acc_ref, *, block_q, block_k, heads_per_step):
    qi = pl.program_id(2)
    heads = range(heads_per_step)
    cols = [slice(hh * HEAD_DIM, (hh + 1) * HEAD_DIM) for hh in heads]
    nkb = block_q // block_k

    def chunk(s0, diagonal, state):
        if diagonal:
            row = lax.broadcasted_iota(jnp.int32, (block_q, block_k), 0)
            lane = lax.broadcasted_iota(jnp.int32, (block_q, block_k), 1)
        zs = [_dot_nt(q_ref[:, cols[hh]], k_ref[pl.ds(s0, block_q), cols[hh]]) for hh in heads]
        carries, probs = [], []
        for hh in heads:
            c = state[hh][0]
            a_blocks = [None] * nkb
            for kb in reversed(range(nkb)):
                zb = zs[hh][:, kb * block_k:(kb + 1) * block_k]
                nz = pltpu.bitcast(pltpu.bitcast(zb, jnp.int32) | jnp.int32(-2 ** 31), F32)
                lg = jnp.log2(1.0 + jnp.exp2(nz))
                sp = jnp.maximum(zb, 0.0) + lg
                ls = jnp.minimum(zb, 0.0) - lg
                if diagonal:
                    strict = (kb * block_k + lane) < row
                    sp = jnp.where(strict, sp, 0.0)
                cs = _dot(sp.astype(BF16), uo_ref[...])
                a = jnp.exp2(ls - cs[:, :block_k] - c)
                if diagonal:
                    a = jnp.where(strict, a, 0.0)
                a_blocks[kb] = a.astype(BF16)
                c = c + cs[:, block_k:]
            carries.append(c)
            probs.append(jnp.concatenate(a_blocks, axis=1))
        return [(carries[hh], state[hh][1] + _dot(probs[hh], v_ref[pl.ds(s0, block_q), cols[hh]]))
                for hh in heads]

    zeros = jnp.zeros((block_q, block_k), F32)
    first = chunk(pl.multiple_of(qi * block_q, block_q), True, [(zeros, zeros)] * heads_per_step)
    for hh in heads:
        c_ref[hh] = first[hh][0]
        acc_ref[hh] = first[hh][1]

    def body(it, carry):
        s0 = pl.multiple_of((qi - 1 - it) * block_q, block_q)
        new = chunk(s0, False, [(c_ref[hh], acc_ref[hh]) for hh in heads])
        for hh in heads:
            c_ref[hh] = new[hh][0]
            acc_ref[hh] = new[hh][1]
        return carry

    lax.fori_loop(0, qi, body, 0)
    for hh in heads:
        o_ref[:, cols[hh]] = acc_ref[hh].astype(o_ref.dtype)


def _sba_core(qkv, batch, seq, block_q=512, block_k=128, heads_per_step=4):
    assert block_k == HEAD_DIM
    m = qkv.shape[0]
    bq = min(block_q, seq)
    hb = heads_per_step
    w = hb * HEAD_DIM
    sec = D_MODEL // w
    qt = seq // bq
    jj = np.arange(block_k)
    uo = np.concatenate([(jj[:, None] > jj[None, :]), np.ones((block_k, block_k), bool)], axis=1)
    uo = jnp.asarray(uo.astype(np.float32), BF16)
    kern = functools.partial(_sba_kernel, block_q=bq, block_k=block_k, heads_per_step=hb)
    return pl.pallas_call(
        kern,
        grid=(batch, N_HEADS // hb, qt),
        in_specs=[
            pl.BlockSpec((bq, w), lambda b, h, i: (b * qt + i, h)),
            pl.BlockSpec((seq, w), lambda b, h, i: (b, sec + h)),
            pl.BlockSpec((seq, w), lambda b, h, i: (b, 2 * sec + h)),
            pl.BlockSpec(uo.shape, lambda b, h, i: (0, 0)),
        ],
        out_specs=pl.BlockSpec((bq, w), lambda b, h, i: (b * qt + i, h)),
        out_shape=jax.ShapeDtypeStruct((m, D_MODEL), BF16),
        scratch_shapes=[pltpu.VMEM((hb, bq, block_k), F32), pltpu.VMEM((hb, bq, HEAD_DIM), F32)],
        compiler_params=_cparams(3),
        name="sba_core",
    )(qkv, qkv, qkv, uo)


def _pool_kernel(h_ref, g0_ref, pw_ref, ps_ref, g1_ref, o_ref, ubuf_ref, y_ref,
                 *, tm, tiles_per_seq):
    ti = pl.program_id(0) % tiles_per_seq
    hist = POOL_HIST

    @pl.when(ti == 0)
    def _():
        ubuf_ref[0:hist, :] = jnp.zeros((hist, D_MODEL), F32)

    h = h_ref[...]
    ubuf_ref[hist:hist + tm, :] = _rms(h, g0_ref[...])
    pos = ti * tm + lax.broadcasted_iota(jnp.int32, (tm, POOL_GROUP), 0) + 1
    for gi, win in enumerate(POOL_WINDOWS):
        cols = slice(gi * POOL_GROUP, (gi + 1) * POOL_GROUP)
        u = ubuf_ref[hist:hist + tm, cols]
        ws = u
        for dlt in range(1, win):
            ws = ws + ubuf_ref[hist - dlt:hist - dlt + tm, cols]
        p = ws / jnp.minimum(pos, win).astype(F32) - u
        y_ref[:, cols] = _dot(p.astype(BF16), pw_ref[gi]) * ps_ref[:, cols]
    o_ref[...] = h + _rms(y_ref[...], g1_ref[...])
    ubuf_ref[0:hist, :] = ubuf_ref[tm:tm + hist, :]


def _pool_layer(h, g0, pool_w, layer, pool_scale, g1, seq, tm=256):
    m, d = h.shape
    tm = min(tm, seq)
    kern = functools.partial(_pool_kernel, tm=tm, tiles_per_seq=seq // tm)
    row = pl.BlockSpec((1, d), lambda i: (0, 0))
    return pl.pallas_call(
        kern,
        grid=(m // tm,),
        in_specs=[
            pl.BlockSpec((tm, d), lambda i: (i, 0)),
            row,
            pl.BlockSpec((None,) + pool_w.shape[1:], lambda i: (layer, 0, 0, 0)),
            row,
            row,
        ],
        out_specs=pl.BlockSpec((tm, d), lambda i: (i, 0)),
        out_shape=jax.ShapeDtypeStruct((m, d), F32),
        scratch_shapes=[pltpu.VMEM((tm + POOL_HIST, d), F32), pltpu.VMEM((tm, d), F32)],
        compiler_params=_cparams(1),
        name="pool_layer",
    )(h, g0, pool_w, pool_scale, g1)


def _ffn_kernel(h_ref, g2_ref, wg_ref, wv_ref, cwg_ref, cwv_ref, cbg_ref, cbv_ref, wd_ref, g3_ref,
                o_ref, xn_ref, acc_ref, ug_ref, uv_ref, carry_ref, *, tm, tf, tiles_per_seq):
    i = pl.program_id(0)
    j = pl.program_id(1)
    hist = CONV_HIST
    first = (i % tiles_per_seq) == 0
    rp = tm // FFN_ROW_PARTS
    parts = [slice(p * rp, (p + 1) * rp) for p in range(FFN_ROW_PARTS)]
    last = pl.num_programs(1) - 1

    def conv(u_ref, cw_ref, cb_ref, r0):
        return (cw_ref[0:1, :] * u_ref[r0 - 2:r0 - 2 + rp, :] + cw_ref[1:2, :] * u_ref[r0 - 1:r0 - 1 + rp, :]
                + cw_ref[2:3, :] * u_ref[r0:r0 + rp, :] + cb_ref[...])

    def step(is_first, is_last):
        if is_first:
            for rows in parts:
                xn_ref[rows, :] = _rms(h_ref[rows, :], g2_ref[...]).astype(BF16)
        ug_ref[0:hist, :] = jnp.where(first, 0.0, carry_ref[j, :, 0:tf])
        uv_ref[0:hist, :] = jnp.where(first, 0.0, carry_ref[j, :, tf:2 * tf])
        for p, rows in enumerate(parts):
            ug_ref[hist + p * rp:hist + (p + 1) * rp, :] = _dot(xn_ref[rows, :], wg_ref[...])
            uv_ref[hist + p * rp:hist + (p + 1) * rp, :] = _dot(xn_ref[rows, :], wv_ref[...])
        carry_ref[j, :, 0:tf] = ug_ref[tm:tm + hist, :]
        carry_ref[j, :, tf:2 * tf] = uv_ref[tm:tm + hist, :]
        for p, rows in enumerate(parts):
            r0 = hist + p * rp
            act = (_silu(conv(ug_ref, cwg_ref, cbg_ref, r0)) * conv(uv_ref, cwv_ref, cbv_ref, r0)).astype(BF16)
            y = _dot(act, wd_ref[...])
            if is_first:
                acc_ref[rows, :] = y
            elif is_last:
                o_ref[rows, :] = h_ref[rows, :] + _rms(acc_ref[rows, :] + y, g3_ref[...])
            else:
                acc_ref[rows, :] += y

    pl.when(j == 0)(functools.partial(step, True, False))
    pl.when(jnp.logical_and(j > 0, j < last))(functools.partial(step, False, False))
    pl.when(j == last)(functools.partial(step, False, True))


def _ffn_layer(h, g2, w_up, conv_w, conv_b, w_down, layer, g3, seq, tm=512, tf=512):
    m, d = h.shape
    tm = min(tm, seq)
    nf = D_FF // tf
    kern = functools.partial(_ffn_kernel, tm=tm, tf=tf, tiles_per_seq=seq // tm)
    row = pl.BlockSpec((1, d), lambda i, j: (0, 0))

    def up_side(rows, half):
        return pl.BlockSpec((None, rows, tf), lambda i, j: (layer, 0, half * nf + j))

    return pl.pallas_call(
        kern,
        grid=(m // tm, nf),
        in_specs=[
            pl.BlockSpec((tm, d), lambda i, j: (i, 0)),
            row,
            up_side(d, 0), up_side(d, 1),
            up_side(3, 0), up_side(3, 1),
            up_side(1, 0), up_side(1, 1),
            pl.BlockSpec((None, tf, d), lambda i, j: (layer, j, 0)),
            row,
        ],
        out_specs=pl.BlockSpec((tm, d), lambda i, j: (i, 0)),
        out_shape=jax.ShapeDtypeStruct((m, d), F32),
        scratch_shapes=[
            pltpu.VMEM((tm, d), BF16),
            pltpu.VMEM((tm, d), F32),
            pltpu.VMEM((tm + CONV_HIST, tf), F32),
            pltpu.VMEM((tm + CONV_HIST, tf), F32),
            pltpu.VMEM((nf, CONV_HIST, 2 * tf), F32),
        ],
        compiler_params=_cparams(2),
        name="ffn_layer",
    )(h, g2, w_up, w_up, conv_w, conv_w, conv_b, conv_b, w_down, g3)


def kernel(x, norm_g, hgrn_lb_logits, hgrn_w_in, hgrn_onorm_g, hgrn_w_out, sba_w_qkv, sba_w_out,
           pool_w, pool_scale, ffn_w_up, ffn_conv_w, ffn_conv_b, ffn_w_down):
    batch, seq, d = x.shape
    depth = norm_g.shape[0]
    lb_soft = jax.nn.softmax(hgrn_lb_logits.astype(F32), axis=0)
    lower = jnp.concatenate([jnp.zeros_like(lb_soft[:1]), jnp.cumsum(lb_soft[1:], axis=0)], axis=0)
    log_lb = jnp.log(lower)
    log1m_lb = jnp.log1p(-lower)

    hgrn_w_in, hgrn_w_out, sba_w_qkv, sba_w_out, pool_w, ffn_w_up, ffn_w_down = (
        w.astype(BF16) for w in (hgrn_w_in, hgrn_w_out, sba_w_qkv, sba_w_out, pool_w, ffn_w_up, ffn_w_down))
    ffn_conv_b = ffn_conv_b[:, None, :]

    h = x.reshape(batch * seq, d)
    for i in range(depth):
        kind, j = i % 3, i // 3
        g = norm_g[i].reshape(4, 1, d)
        if kind == 0:
            proj = _norm_matmul(h, g[0], hgrn_w_in, j, F32)
            mix = _hgrn_core(proj, log_lb[i:i + 1], log1m_lb[i:i + 1], hgrn_onorm_g[j][None, :], batch, seq)
            h = _proj_residual(mix, hgrn_w_out, j, g[1], h)
        elif kind == 1:
            qkv = _norm_matmul(h, g[0], sba_w_qkv, j, BF16,
                               scaled_cols=D_MODEL, scale=float(HEAD_DIM) ** -0.5 * LOG2_E)
            mix = _sba_core(qkv, batch, seq)
            h = _proj_residual(mix, sba_w_out, j, g[1], h)
        else:
            h = _pool_layer(h, g[0], pool_w, j, pool_scale[j][None, :], g[1], seq)
        h = _ffn_layer(h, g[2], ffn_w_up, ffn_conv_w, ffn_conv_b, ffn_w_down, i, g[3], seq)
    return h.reshape(batch, seq, d)
```
